```python
import math
import jax, jax.numpy as jnp
from jax import lax
import numpy as np

D_MODEL = 1024
BATCH = 8
SEQ = 2048
DEPTH = 2
DEC_BATCH = 128
DEC_SEQ = 1
PAST_LEN = 2048
PAGE_SIZE = 128

GLA_HEADS = 4
GLA_DK = D_MODEL // 2
GLA_DV = D_MODEL
GLA_HDK = GLA_DK // GLA_HEADS
GLA_HDV = GLA_DV // GLA_HEADS
GLA_GATE_RANK = 16
GLA_GATE_TEMP = 16.0
GLA_LOGA_MIN = -4.0
GLA_CHUNK = 16
MOBA_HEADS = 8
MOBA_HD = D_MODEL // MOBA_HEADS
MOBA_D = MOBA_HEADS * MOBA_HD
MOBA_BLOCK = 256
MOBA_TOPK = 3
MOBA_QBLOCK = 8
ROT_DIM = MOBA_HD // 4
ROPE_THETA = 500000.0
D_FF = 2816
FFN_RES = 0.5
NORM_EPS = 1e-6
IN_SIZES = (GLA_DK, GLA_DK, GLA_DV, GLA_DV, GLA_GATE_RANK, MOBA_D, MOBA_D, MOBA_D, D_MODEL, D_MODEL)
IN_COLS = 2 * GLA_DK + 2 * GLA_DV + GLA_GATE_RANK + 3 * MOBA_D + 2 * D_MODEL

kernel_name = 'hybrid_gla_moba_macaron_step'


def rmsnorm(x, g):
    xf = x.astype(jnp.float32)
    y = xf * lax.rsqrt(jnp.mean(xf * xf, axis=-1, keepdims=True) + NORM_EPS)
    return (y * g.astype(jnp.float32)).astype(x.dtype)


def swiglu(x, w_gate, w_up, w_down):
    return (jax.nn.silu(x @ w_gate) * (x @ w_up)) @ w_down


def rope_partial(x, pos):
    half = ROT_DIM // 2
    inv = ROPE_THETA ** (-jnp.arange(half, dtype=jnp.float32) * 2.0 / ROT_DIM)
    ang = pos.astype(jnp.float32)[:, None] * inv[None, :]
    cos = jnp.cos(ang)[None, :, None, :]
    sin = jnp.sin(ang)[None, :, None, :]
    xf = x.astype(jnp.float32)
    x1, x2 = xf[..., :half], xf[..., half:ROT_DIM]
    out = jnp.concatenate([x1 * cos - x2 * sin, x2 * cos + x1 * sin, xf[..., ROT_DIM:]], axis=-1)
    return out.astype(x.dtype)


def gla_scan(q, k, v, log_a, s0):
    B, T, H, _ = q.shape
    C = GLA_CHUNK
    pad = (-T) % C
    nc = (T + pad) // C

    def prep(a):
        a = jnp.pad(a.astype(jnp.float32), ((0, 0), (0, pad), (0, 0), (0, 0)))
        return a.reshape(B, nc, C, H, a.shape[-1]).transpose(1, 0, 3, 2, 4)

    qc = prep(q) * (GLA_HDK ** -0.5)
    kc, vc = prep(k), prep(v)
    bc = jnp.cumsum(prep(log_a), axis=3)
    causal = jnp.tril(jnp.ones((C, C), dtype=bool))

    def body(S, inp):
        qi, ki, vi, bi = inp
        qt = qi * jnp.exp(bi)
        kt = ki * jnp.exp(-bi)
        att = jnp.where(causal, jnp.einsum('bhid,bhjd->bhij', qt, kt), 0.0)
        o = jnp.einsum('bhij,bhjv->bhiv', att, vi) + jnp.einsum('bhid,bhdv->bhiv', qt, S)
        blast = bi[:, :, -1:, :]
        S = jnp.exp(blast[:, :, 0, :, None]) * S + jnp.einsum('bhjd,bhjv->bhdv', ki * jnp.exp(blast - bi), vi)
        return S, o

    S, o = lax.scan(body, s0.astype(jnp.float32), (qc, kc, vc, bc))
    o = o.transpose(1, 0, 3, 2, 4).reshape(B, nc * C, H, -1)[:, :T]
    return o, S


def moba_attention(q, k, v, q_pos0):
    B, Tq, H, hd = q.shape
    nblk = k.shape[1] // MOBA_BLOCK
    kb = k.reshape(B, nblk, MOBA_BLOCK, H, hd)
    vb = v.reshape(B, nblk, MOBA_BLOCK, H, hd)
    means = jnp.mean(kb.astype(jnp.float32), axis=2)
    topk = min(MOBA_TOPK, nblk)
    qb = math.gcd(Tq, MOBA_QBLOCK)
    nqb = Tq // qb
    bidx = jnp.arange(B)[:, None, None, None]
    hidx = jnp.arange(H)[None, None, :, None]
    koff = jnp.arange(MOBA_BLOCK)
    scale = hd ** -0.5

    def step(i):
        qs = lax.dynamic_slice_in_dim(q, i * qb, qb, axis=1).astype(jnp.float32)
        qpos = q_pos0 + i * qb + jnp.arange(qb)
        own = qpos // MOBA_BLOCK
        gate = jnp.einsum('bqhd,bnhd->bqhn', qs, means)
        past = jnp.arange(nblk)[None, :] < own[:, None]
        gate = jnp.where(past[None, :, None, :], gate, -jnp.inf)
        gval, gidx = lax.top_k(gate, topk)
        idx = jnp.concatenate([gidx, jnp.broadcast_to(own[None, :, None, None], (B, qb, H, 1))], axis=-1)
        slot_ok = jnp.concatenate([jnp.isfinite(gval), jnp.ones((B, qb, H, 1), dtype=bool)], axis=-1)
        kg = kb[bidx, idx, :, hidx].astype(jnp.float32)
        vg = vb[bidx, idx, :, hidx].astype(jnp.float32)
        s = jnp.einsum('bqhd,bqhnkd->bqhnk', qs, kg) * scale
        kpos = idx[..., None] * MOBA_BLOCK + koff
        mask = slot_ok[..., None] & (kpos <= qpos[None, :, None, None, None])
        s = jnp.where(mask, s, -jnp.inf).reshape(B, qb, H, -1)
        p = jax.nn.softmax(s, axis=-1).reshape(B, qb, H, topk + 1, MOBA_BLOCK)
        return jnp.einsum('bqhnk,bqhnkd->bqhd', p, vg).astype(q.dtype)

    o = lax.map(step, jnp.arange(nqb))
    return o.transpose(1, 0, 2, 3, 4).reshape(B, Tq, H, hd)


def pad_rows_to_block(rows):
    L = sum(r.shape[1] for r in rows)
    pad = (-L) % MOBA_BLOCK
    r0 = rows[0]
    zeros = jnp.zeros((r0.shape[0], pad, r0.shape[2], r0.shape[3]), dtype=r0.dtype)
    return jnp.concatenate(list(rows) + [zeros], axis=1)


def trunk_layer(x, pos0, k_past, v_past, s0,
                ffn1_norm, ffn1_w_gate, ffn1_w_up, ffn1_w_down,
                mix_norm, w_in, gla_w_alpha, gla_b_alpha, gla_out_norm,
                w_o_gla, w_o_moba, w_out,
                ffn2_norm, ffn2_w_gate, ffn2_w_up, ffn2_w_down):
    B, T, _ = x.shape
    x = x + FFN_RES * swiglu(rmsnorm(x, ffn1_norm), ffn1_w_gate, ffn1_w_up, ffn1_w_down)
    h = rmsnorm(x, mix_norm)
    proj = h @ w_in
    points = [int(p) for p in np.cumsum(IN_SIZES)[:-1]]
    gq, gk, gv, gr, ga, mq, mk, mv, gate_a, gate_b = jnp.split(proj, points, axis=-1)
    log_a = jax.nn.log_sigmoid((ga @ gla_w_alpha + gla_b_alpha).astype(jnp.float32)) / GLA_GATE_TEMP
    log_a = jnp.maximum(log_a, GLA_LOGA_MIN)
    o_gla, s_new = gla_scan(gq.reshape(B, T, GLA_HEADS, GLA_HDK), gk.reshape(B, T, GLA_HEADS, GLA_HDK),
                            gv.reshape(B, T, GLA_HEADS, GLA_HDV), log_a.reshape(B, T, GLA_HEADS, GLA_HDK), s0)
    o_gla = rmsnorm(o_gla.astype(x.dtype), gla_out_norm) * jax.nn.silu(gr.reshape(B, T, GLA_HEADS, GLA_HDV))
    y_a = o_gla.reshape(B, T, GLA_DV) @ w_o_gla
    pos = pos0 + jnp.arange(T)
    q_m = rope_partial(mq.reshape(B, T, MOBA_HEADS, MOBA_HD), pos)
    k_m = rope_partial(mk.reshape(B, T, MOBA_HEADS, MOBA_HD), pos)
    v_m = mv.reshape(B, T, MOBA_HEADS, MOBA_HD)
    if k_past is None:
        k_all = pad_rows_to_block([k_m])
        v_all = pad_rows_to_block([v_m])
    else:
        k_all = pad_rows_to_block([k_past.astype(k_m.dtype), k_m])
        v_all = pad_rows_to_block([v_past.astype(v_m.dtype), v_m])
    o_moba = moba_attention(q_m, k_all, v_all, pos0)
    y_b = o_moba.reshape(B, T, MOBA_D) @ w_o_moba
    merged = jax.nn.sigmoid(gate_a) * y_a + jax.nn.sigmoid(gate_b) * y_b
    x = x + merged @ w_out
    x = x + FFN_RES * swiglu(rmsnorm(x, ffn2_norm), ffn2_w_gate, ffn2_w_up, ffn2_w_down)
    return x, k_m, v_m, s_new


def setup_inputs(seed: int = 0) -> dict:
    key = jax.random.key(seed)
    ks = jax.random.split(key, 32)
    n_pages = PAST_LEN // PAGE_SIZE
    n_pool = (DEC_BATCH * n_pages * 5) // 4
    f32 = jnp.float32

    def w(k, shape, fan_in):
        return jax.random.normal(k, shape, f32) * (fan_in ** -0.5)

    def gain(k, shape):
        return 1.0 + 0.02 * jax.random.normal(k, shape, f32)

    page_table = jax.random.permutation(ks[6], n_pool)[:DEC_BATCH * n_pages].reshape(DEC_BATCH, n_pages).astype(jnp.int32)
    return {
        'x_prompt': jax.random.normal(ks[0], (BATCH, SEQ, D_MODEL), f32),
        'x_sample': jax.random.normal(ks[1], (DEC_BATCH, DEC_SEQ, D_MODEL), f32),
        'cache_k': jax.random.normal(ks[2], (DEPTH, n_pool, PAGE_SIZE, MOBA_HEADS, MOBA_HD), f32),
        'cache_v': jax.random.normal(ks[3], (DEPTH, n_pool, PAGE_SIZE, MOBA_HEADS, MOBA_HD), f32),
        'state_gla': 0.5 * jax.random.normal(ks[4], (DEPTH, DEC_BATCH, GLA_HEADS, GLA_HDK, GLA_HDV), f32),
        'page_table': page_table,
        'ffn1_norm': gain(ks[7], (DEPTH, D_MODEL)),
        'ffn1_w_gate': w(ks[8], (DEPTH, D_MODEL, D_FF), D_MODEL),
        'ffn1_w_up': w(ks[9], (DEPTH, D_MODEL, D_FF), D_MODEL),
        'ffn1_w_down': w(ks[10], (DEPTH, D_FF, D_MODEL), D_FF),
        'mix_norm': gain(ks[11], (DEPTH, D_MODEL)),
        'w_in': w(ks[12], (DEPTH, D_MODEL, IN_COLS), D_MODEL),
        'gla_w_alpha': w(ks[13], (DEPTH, GLA_GATE_RANK, GLA_DK), GLA_GATE_RANK),
        'gla_b_alpha': 0.1 * jax.random.normal(ks[14], (DEPTH, GLA_DK), f32),
        'gla_out_norm': gain(ks[15], (DEPTH, GLA_HDV)),
        'w_o_gla': w(ks[16], (DEPTH, GLA_DV, D_MODEL), GLA_DV),
        'w_o_moba': w(ks[17], (DEPTH, MOBA_D, D_MODEL), MOBA_D),
        'w_out': w(ks[18], (DEPTH, D_MODEL, D_MODEL), D_MODEL),
        'ffn2_norm': gain(ks[19], (DEPTH, D_MODEL)),
        'ffn2_w_gate': w(ks[20], (DEPTH, D_MODEL, D_FF), D_MODEL),
        'ffn2_w_up': w(ks[21], (DEPTH, D_MODEL, D_FF), D_MODEL),
        'ffn2_w_down': w(ks[22], (DEPTH, D_FF, D_MODEL), D_FF),
        'final_norm': gain(ks[23], (D_MODEL,)),
    }


def reference(x_prompt, x_sample, cache_k, cache_v, state_gla, page_table,
              ffn1_norm, ffn1_w_gate, ffn1_w_up, ffn1_w_down,
              mix_norm, w_in, gla_w_alpha, gla_b_alpha, gla_out_norm,
              w_o_gla, w_o_moba, w_out,
              ffn2_norm, ffn2_w_gate, ffn2_w_up, ffn2_w_down, final_norm):
    bp = x_prompt.shape[0]
    bs, n_pages = page_table.shape
    past_len = n_pages * cache_k.shape[2]
    xp, xs = x_prompt, x_sample
    kp_l, vp_l, sp_l, ks_l, vs_l, ss_l = [], [], [], [], [], []
    for l in range(DEPTH):
        lw = (ffn1_norm[l], ffn1_w_gate[l], ffn1_w_up[l], ffn1_w_down[l],
              mix_norm[l], w_in[l], gla_w_alpha[l], gla_b_alpha[l], gla_out_norm[l],
              w_o_gla[l], w_o_moba[l], w_out[l],
              ffn2_norm[l], ffn2_w_gate[l], ffn2_w_up[l], ffn2_w_down[l])
        s0 = jnp.zeros((bp, GLA_HEADS, GLA_HDK, GLA_HDV), jnp.float32)
        xp, kp, vp, sp = trunk_layer(xp, 0, None, None, s0, *lw)
        k_past = cache_k[l][page_table].reshape(bs, past_len, MOBA_HEADS, MOBA_HD)
        v_past = cache_v[l][page_table].reshape(bs, past_len, MOBA_HEADS, MOBA_HD)
        xs, ksm, vsm, ssm = trunk_layer(xs, past_len, k_past, v_past, state_gla[l], *lw)
        kp_l.append(kp); vp_l.append(vp); sp_l.append(sp)
        ks_l.append(ksm); vs_l.append(vsm); ss_l.append(ssm)
    y_prompt = rmsnorm(xp, final_norm)
    y_sample = rmsnorm(xs, final_norm)
    new_k_prompt = jnp.stack(kp_l)
    new_v_prompt = jnp.stack(vp_l)
    new_state_prompt = jnp.stack(sp_l)
    new_k_sample = jnp.stack(ks_l)
    new_v_sample = jnp.stack(vs_l)
    new_state_sample = jnp.stack(ss_l)
    return (y_prompt, y_sample, new_k_prompt, new_v_prompt, new_state_prompt, new_k_sample, new_v_sample, new_state_sample)
```

```python
import functools

import jax
import jax.numpy as jnp
from jax import lax
from jax.experimental import pallas as pl
from jax.experimental.pallas import tpu as pltpu

F32 = jnp.float32
BF16 = jnp.bfloat16

D_MODEL = 1024
GLA_HEADS = 4
GLA_DK = D_MODEL // 2
GLA_DV = D_MODEL
GLA_HDK = GLA_DK // GLA_HEADS
GLA_HDV = GLA_DV // GLA_HEADS
GLA_GATE_RANK = 16
GLA_GATE_TEMP = 16.0
GLA_LOGA_MIN = -4.0
MOBA_HEADS = 8
MOBA_HD = D_MODEL // MOBA_HEADS
MOBA_D = MOBA_HEADS * MOBA_HD
MOBA_BLOCK = 256
MOBA_TOPK = 3
ROT_DIM = MOBA_HD // 4
ROPE_THETA = 500000.0
D_FF = 2816
FFN_RES = 0.5
NORM_EPS = 1e-6

LANES = 128
FFN_CHUNK = 256
GLA_CHUNK = 32
GLA_MID = GLA_CHUNK // 2 - 1
NEG = -1e30
VMEM_LIMIT = 56 * 1024 * 1024

_NT = (((1,), (1,)), ((), ()))
_TN = (((0,), (0,)), ((), ()))


def _rms(x, g):
    return x * lax.rsqrt(jnp.mean(x * x, axis=-1, keepdims=True) + NORM_EPS) * g


def _dot(a, b):
    return jnp.dot(a, b, preferred_element_type=F32)


def _resident(shape):
    nd = len(shape)
    return pl.BlockSpec(shape, lambda *_: (0,) * nd, pipeline_mode=pl.Buffered(1))


def _params(*sem):
    return pltpu.CompilerParams(dimension_semantics=sem, vmem_limit_bytes=VMEM_LIMIT)


def _ffn_kernel(x_ref, g_ref, wg_ref, wu_ref, wd_ref, fn_ref, o_ref, a_scr, *, final_norm):
    x = x_ref[...]
    h = _rms(x, g_ref[...]).astype(BF16)
    for c in range(D_FF // FFN_CHUNK):
        sl = slice(c * FFN_CHUNK, (c + 1) * FFN_CHUNK)
        g = _dot(h, wg_ref[:, sl])
        u = _dot(h, wu_ref[:, sl])
        a_scr[:, sl] = (g * jax.nn.sigmoid(g) * u).astype(BF16)
    y = x + FFN_RES * _dot(a_scr[...], wd_ref[...])
    if final_norm:
        y = _rms(y, fn_ref[...])
    o_ref[...] = y


def _ffn(x, g, wg, wu, wd, fn, *, tm, final_norm):
    n = x.shape[0]
    row = pl.BlockSpec((tm, D_MODEL), lambda i: (i, 0))
    return pl.pallas_call(
        functools.partial(_ffn_kernel, final_norm=final_norm),
        grid=(n // tm,),
        in_specs=[row, _resident((1, D_MODEL)), _resident((D_MODEL, D_FF)),
                  _resident((D_MODEL, D_FF)), _resident((D_FF, D_MODEL)),
                  _resident((1, D_MODEL))],
        out_specs=row,
        out_shape=jax.ShapeDtypeStruct((n, D_MODEL), F32),
        scratch_shapes=[pltpu.VMEM((tm, D_FF), BF16)],
        compiler_params=_params("parallel"),
        name="ffn",
    )(x, g, wg, wu, wd, fn)


def _rope(y, ra, rb1, rb2):
    half = ROT_DIM // 2
    return (y * ra + pltpu.roll(y, LANES - half, 1) * rb1 + pltpu.roll(y, half, 1) * rb2)


def _inproj_kernel(x_ref, g_ref, wgq_ref, wgk_ref, wgv_ref, wgr_ref, wga_ref, wal_ref, bal_ref,
                   wmq_ref, wmk_ref, wmv_ref, wa_ref, wb_ref, ra_ref, rb1_ref, rb2_ref, tri_ref,
                   gq_ref, gk_ref, gv_ref, sgr_ref, bl_ref, mq_ref, mk_ref, mv_ref, sa_ref, sb_ref,
                   *, cum):
    h = _rms(x_ref[...], g_ref[...]).astype(BF16)
    gq_ref[...] = _dot(h, wgq_ref[...])
    gk_ref[...] = _dot(h, wgk_ref[...])
    gv_ref[...] = _dot(h, wgv_ref[...])
    gr = _dot(h, wgr_ref[...])
    sgr_ref[...] = gr * jax.nn.sigmoid(gr)
    ga = _dot(h, wga_ref[...]).astype(BF16)
    z = _dot(ga, wal_ref[...]) + bal_ref[...]
    log_a = (jnp.minimum(z, 0.0) - jnp.log(1.0 + jnp.exp(-jnp.abs(z)))) * (1.0 / GLA_GATE_TEMP)
    log_a = jnp.maximum(log_a, GLA_LOGA_MIN)
    if cum:
        hi = log_a.astype(BF16)
        lo = (log_a - hi.astype(F32)).astype(BF16)
        tri = tri_ref[...]
        bl_ref[...] = _dot(tri, hi) + _dot(tri, lo)
    else:
        bl_ref[...] = log_a
    ra, rb1, rb2 = ra_ref[...], rb1_ref[...], rb2_ref[...]
    mq = _dot(h, wmq_ref[...])
    mk = _dot(h, wmk_ref[...])
    for hd in range(MOBA_HEADS):
        sl = slice(hd * MOBA_HD, (hd + 1) * MOBA_HD)
        mq_ref[:, sl] = _rope(mq[:, sl], ra, rb1, rb2)
        mk_ref[:, sl] = _rope(mk[:, sl], ra, rb1, rb2)
    mv_ref[...] = _dot(h, wmv_ref[...])
    sa_ref[...] = jax.nn.sigmoid(_dot(h, wa_ref[...]))
    sb_ref[...] = jax.nn.sigmoid(_dot(h, wb_ref[...]))


def _inproj(x, g, w, rope, tri, *, tm, n_pos_tiles, cum):
    n = x.shape[0]
    row = lambda width: pl.BlockSpec((tm, width), lambda i: (i, 0))
    pos = pl.BlockSpec((tm, LANES), lambda i: (i % n_pos_tiles, 0))
    widths = (GLA_DK, GLA_DK, GLA_DV, GLA_DV, GLA_DK, MOBA_D, MOBA_D, MOBA_D, D_MODEL, D_MODEL)
    return pl.pallas_call(
        functools.partial(_inproj_kernel, cum=cum),
        grid=(n // tm,),
        in_specs=[row(D_MODEL), _resident((1, D_MODEL))]
        + [_resident(a.shape) for a in w]
        + [pos, pos, pos, _resident(tri.shape)],
        out_specs=[row(wd) for wd in widths],
        out_shape=[jax.ShapeDtypeStruct((n, wd), F32) for wd in widths],
        compiler_params=_params("parallel"),
        name="inproj",
    )(x, g, *w, *rope, tri)


def _gla_prompt_kernel(q_ref, k_ref, b_ref, v_ref, sgr_ref, gn_ref, o_ref, s_ref, st_scr):
    t = q_ref.shape[0]
    c_len = GLA_CHUNK
    st_scr[...] = jnp.zeros_like(st_scr)
    causal = (lax.broadcasted_iota(jnp.int32, (c_len, c_len), 0)
              >= lax.broadcasted_iota(jnp.int32, (c_len, c_len), 1))
    scale = GLA_HDK ** -0.5
    gn = gn_ref[...]

    def body(c, carry):
        rows = pl.ds(pl.multiple_of(c * c_len, c_len), c_len)
        q, k, b, v = q_ref[rows, :], k_ref[rows, :], b_ref[rows, :], v_ref[rows, :]
        b_mid = b[GLA_MID:GLA_MID + 1, :]
        b_last = b[c_len - 1:c_len, :]
        qt = q * jnp.exp(b - b_mid) * scale
        kt = k * jnp.exp(b_mid - b)
        att = lax.dot_general(qt.astype(BF16), kt.astype(BF16), _NT, preferred_element_type=F32)
        att = jnp.where(causal, att, 0.0)
        qs = (qt * jnp.exp(b_mid)).astype(BF16)
        ke = (kt * jnp.exp(b_last - b_mid)).astype(BF16)
        vb = v.astype(BF16)
        st = st_scr[...]
        o = _dot(att.astype(BF16), vb) + lax.dot_general(
            qs, st.astype(BF16), _NT, preferred_element_type=F32)
        st_scr[...] = st * jnp.exp(b_last) + lax.dot_general(
            vb, ke, _TN, preferred_element_type=F32)
        o_ref[rows, :] = _rms(o, gn) * sgr_ref[rows, :]
        return carry

    lax.fori_loop(0, t // c_len, body, 0)
    s_ref[0, 0] = st_scr[...].T


def _gla_prompt(gq, gk, bl, gv, sgr, gn, *, batch, t):
    n = gq.shape[0]
    kblk = pl.BlockSpec((t, GLA_HDK), lambda b, h: (b, h))
    vblk = pl.BlockSpec((t, GLA_HDV), lambda b, h: (b, h))
    return pl.pallas_call(
        _gla_prompt_kernel,
        grid=(batch, GLA_HEADS),
        in_specs=[kblk, kblk, kblk, vblk, vblk, pl.BlockSpec((1, GLA_HDV), lambda b, h: (0, 0))],
        out_specs=[vblk, pl.BlockSpec((1, 1, GLA_HDK, GLA_HDV), lambda b, h: (b, h, 0, 0))],
        out_shape=[jax.ShapeDtypeStruct((n, GLA_DV), F32),
                   jax.ShapeDtypeStruct((batch, GLA_HEADS, GLA_HDK, GLA_HDV), F32)],
        scratch_shapes=[pltpu.VMEM((GLA_HDV, GLA_HDK), F32)],
        compiler_params=_params("parallel", "parallel"),
        name="gla_prompt",
    )(gq, gk, bl, gv, sgr, gn)


def _column(r, width):
    col = jnp.broadcast_to(r, (LANES, LANES)).T
    return jnp.concatenate([col] * (width // LANES), axis=1)


def _gla_sample_kernel(q_ref, k_ref, la_ref, v_ref, sgr_ref, gn_ref, s_ref, o_ref, sn_ref):
    scale = GLA_HDK ** -0.5
    gn = gn_ref[...]
    for h in range(GLA_HEADS):
        ks = slice(h * GLA_HDK, (h + 1) * GLA_HDK)
        vs = slice(h * GLA_HDV, (h + 1) * GLA_HDV)
        q, k, la = q_ref[0, :, ks], k_ref[0, :, ks], la_ref[0, :, ks]
        v = v_ref[0, :, vs]
        s_new = _column(jnp.exp(la), GLA_HDV) * s_ref[0, h] + _column(k, GLA_HDV) * v
        sn_ref[0, h] = s_new
        q8 = jnp.broadcast_to(q * scale, (8, GLA_HDK)).astype(BF16)
        o = _dot(q8, s_new.astype(BF16))[0:1, :]
        o_ref[0, :, vs] = _rms(o, gn) * sgr_ref[0, :, vs]


def _gla_sample(gq, gk, la, gv, sgr, gn, state):
    bs = gq.shape[0]
    r3 = lambda a: a.reshape(bs, 1, a.shape[-1])
    kblk = pl.BlockSpec((1, 1, GLA_DK), lambda b: (b, 0, 0))
    vblk = pl.BlockSpec((1, 1, GLA_DV), lambda b: (b, 0, 0))
    sblk = pl.BlockSpec((1, GLA_HEADS, GLA_HDK, GLA_HDV), lambda b: (b, 0, 0, 0))
    o, s_new = pl.pallas_call(
        _gla_sample_kernel,
        grid=(bs,),
        in_specs=[kblk, kblk, kblk, vblk, vblk, pl.BlockSpec((1, GLA_HDV), lambda b: (0, 0)), sblk],
        out_specs=[vblk, sblk],
        out_shape=[jax.ShapeDtypeStruct((bs, 1, GLA_DV), F32),
                   jax.ShapeDtypeStruct(state.shape, F32)],
        compiler_params=_params("parallel"),
        name="gla_sample",
    )(r3(gq), r3(gk), r3(la), r3(gv), r3(sgr), gn, state)
    return o.reshape(bs, GLA_DV), s_new


def _rank_lt_topk(gate, n_valid, lane):
    rank = jnp.zeros_like(gate)
    for jp in range(n_valid):
        col = gate[:, jp:jp + 1]
        ge = jnp.where(col >= gate, 1.0, 0.0)
        gt = jnp.where(col > gate, 1.0, 0.0)
        rank = rank + jnp.where(lane > jp, ge, jnp.where(lane < jp, gt, 0.0))
    return jnp.where(rank < MOBA_TOPK, 1.0, 0.0)


def _moba_prompt_kernel(q_ref, k_ref, v_ref, o_ref):
    t = q_ref.shape[0]
    blk = MOBA_BLOCK
    nb = t // blk
    scale = MOBA_HD ** -0.5
    k = k_ref[...]
    kb = k.astype(BF16)
    vb = v_ref[...].astype(BF16)
    means = jnp.sum(k.reshape(nb, blk, MOBA_HD), axis=1) * (1.0 / blk)
    means = jnp.concatenate([means, jnp.zeros((LANES - nb, MOBA_HD), F32)], axis=0)
    causal = (lax.broadcasted_iota(jnp.int32, (blk, blk), 0)
              >= lax.broadcasted_iota(jnp.int32, (blk, blk), 1))
    lane = lax.broadcasted_iota(jnp.int32, (blk, LANES), 1)
    for i in range(nb):
        rows = slice(i * blk, (i + 1) * blk)
        q = q_ref[rows, :]
        width = (i + 1) * blk
        s = lax.dot_general((q * scale).astype(BF16), kb[:width], _NT, preferred_element_type=F32)
        pieces = []
        if i > MOBA_TOPK:
            gate = lax.dot_general(q, means, _NT, preferred_element_type=F32,
                                   precision=lax.Precision.HIGHEST)
            sel = _rank_lt_topk(gate, i, lane)
            for j in range(i):
                bias = jnp.where(sel[:, j:j + 1] > 0.5, 0.0, NEG)
                pieces.append(s[:, j * blk:(j + 1) * blk] + bias)
        else:
            for j in range(i):
                pieces.append(s[:, j * blk:(j + 1) * blk])
        pieces.append(jnp.where(causal, s[:, i * blk:], NEG))
        sm = jnp.concatenate(pieces, axis=1) if len(pieces) > 1 else pieces[0]
        m = jnp.max(sm, axis=-1, keepdims=True)
        p = jnp.exp(sm - m)
        l = jnp.sum(p, axis=-1, keepdims=True)
        o_ref[rows, :] = _dot(p.astype(BF16), vb[:width]) / l


def _moba_prompt(mq, mk, mv, *, batch, t):
    n = mq.shape[0]
    blk = pl.BlockSpec((t, MOBA_HD), lambda b, h: (b, h))
    return pl.pallas_call(
        _moba_prompt_kernel,
        grid=(batch, MOBA_HEADS),
        in_specs=[blk, blk, blk],
        out_specs=blk,
        out_shape=jax.ShapeDtypeStruct((n, MOBA_D), F32),
        compiler_params=_params("parallel", "parallel"),
        name="moba_prompt",
    )(mq, mk, mv)


def _moba_sample_kernel(pt_ref, q_ref, kn_ref, vn_ref, kp_ref, vp_ref, o_ref,
                        m_scr, l_scr, o_scr, ks_scr, *, pages_per_block):
    del pt_ref
    p = pl.program_id(1)
    n_pages = pl.num_programs(1)
    scale = MOBA_HD ** -0.5
    q = q_ref[0]
    kp = kp_ref[...]
    s = jnp.sum(kp * q[None], axis=-1, keepdims=True) * scale
    m = jnp.max(s, axis=0)
    e = jnp.exp(s - m[None])
    shape = (MOBA_HEADS, MOBA_HD)
    m_scr[p] = jnp.broadcast_to(m, shape)
    l_scr[p] = jnp.broadcast_to(jnp.sum(e, axis=0), shape)
    o_scr[p] = jnp.sum(e * vp_ref[...], axis=0)
    ks_scr[p] = jnp.sum(kp, axis=0)

    @pl.when(p == n_pages - 1)
    def _():
        npg = m_scr.shape[0]
        nblk = npg // pages_per_block
        block_len = pages_per_block * kp_ref.shape[0]
        ksum = ks_scr[...].reshape(nblk, pages_per_block, *shape)
        means = jnp.sum(ksum, axis=1) * (1.0 / block_len)
        gate = jnp.broadcast_to(jnp.sum(means * q[None], axis=-1, keepdims=True), (nblk,) + shape)
        bidx = lax.broadcasted_iota(jnp.int32, (nblk,) + shape, 0)
        rank = jnp.zeros_like(gate)
        for jp in range(nblk):
            g = gate[jp][None]
            ge = jnp.where(g >= gate, 1.0, 0.0)
            gt = jnp.where(g > gate, 1.0, 0.0)
            rank = rank + jnp.where(bidx > jp, ge, jnp.where(bidx < jp, gt, 0.0))
        sel = jnp.where(rank < MOBA_TOPK, 1.0, 0.0)
        sel = jnp.broadcast_to(sel[:, None], (nblk, pages_per_block) + shape)
        sel = sel.reshape((npg,) + shape) > 0.5
        s_self = jnp.broadcast_to(
            jnp.sum(q * kn_ref[0], axis=-1, keepdims=True) * scale, shape)
        mp = jnp.where(sel, m_scr[...], NEG)
        m_all = jnp.maximum(jnp.max(mp, axis=0), s_self)
        w = jnp.where(sel, jnp.exp(mp - m_all[None]), 0.0)
        w_self = jnp.exp(s_self - m_all)
        l_all = w_self + jnp.sum(w * l_scr[...], axis=0)
        o_all = w_self * vn_ref[0] + jnp.sum(w * o_scr[...], axis=0)
        o_ref[0] = o_all / l_all


def _moba_sample(mq, mk, mv, cache_k, cache_v, page_table, layer):
    bs, n_pages = page_table.shape
    page = cache_k.shape[2]
    r3 = lambda a: a.reshape(bs, MOBA_HEADS, MOBA_HD)
    tok = pl.BlockSpec((1, MOBA_HEADS, MOBA_HD), lambda b, p, pt: (b, 0, 0))
    pg = pl.BlockSpec((None, None, page, MOBA_HEADS, MOBA_HD),
                      lambda b, p, pt: (layer, pt[b * n_pages + p], 0, 0, 0))
    stat = pltpu.VMEM((n_pages, MOBA_HEADS, MOBA_HD), F32)
    o = pl.pallas_call(
        functools.partial(_moba_sample_kernel, pages_per_block=MOBA_BLOCK // page),
        grid_spec=pltpu.PrefetchScalarGridSpec(
            num_scalar_prefetch=1,
            grid=(bs, n_pages),
            in_specs=[tok, tok, tok, pg, pg],
            out_specs=tok,
            scratch_shapes=[stat, stat, stat, stat]),
        out_shape=jax.ShapeDtypeStruct((bs, MOBA_HEADS, MOBA_HD), F32),
        compiler_params=_params("parallel", "arbitrary"),
        name="moba_sample",
    )(page_table.reshape(-1), r3(mq), r3(mk), r3(mv), cache_k, cache_v)
    return o.reshape(bs, MOBA_D)


def _outproj_kernel(x_ref, og_ref, om_ref, sa_ref, sb_ref, wg_ref, wm_ref, wo_ref, o_ref):
    ya = _dot(og_ref[...].astype(BF16), wg_ref[...])
    yb = _dot(om_ref[...].astype(BF16), wm_ref[...])
    merged = sa_ref[...] * ya + sb_ref[...] * yb
    o_ref[...] = x_ref[...] + _dot(merged.astype(BF16), wo_ref[...])


def _outproj(x, og, om, sa, sb, wg, wm, wo, *, tm):
    n = x.shape[0]
    row = pl.BlockSpec((tm, D_MODEL), lambda i: (i, 0))
    wspec = _resident((D_MODEL, D_MODEL))
    return pl.pallas_call(
        _outproj_kernel,
        grid=(n // tm,),
        in_specs=[row] * 5 + [wspec] * 3,
        out_specs=row,
        out_shape=jax.ShapeDtypeStruct((n, D_MODEL), F32),
        compiler_params=_params("parallel"),
        name="outproj",
    )(x, og, om, sa, sb, wg, wm, wo)


def _rope_tables(pos):
    half = ROT_DIM // 2
    inv = ROPE_THETA ** (-jnp.arange(half, dtype=F32) * 2.0 / ROT_DIM)
    ang = pos.astype(F32)[:, None] * inv[None, :]
    cos, sin = jnp.cos(ang), jnp.sin(ang)
    n = pos.shape[0]
    zeros = lambda w: jnp.zeros((n, w), F32)
    ra = jnp.concatenate([cos, cos, jnp.ones((n, MOBA_HD - ROT_DIM), F32)], axis=1)
    rb1 = jnp.concatenate([-sin, zeros(MOBA_HD - half)], axis=1)
    rb2 = jnp.concatenate([zeros(half), sin, zeros(MOBA_HD - ROT_DIM)], axis=1)
    return ra, rb1, rb2


def _chunk_tri(tm):
    r = jnp.arange(tm)
    same = (r[:, None] // GLA_CHUNK) == (r[None, :] // GLA_CHUNK)
    return (same & (r[:, None] >= r[None, :])).astype(BF16)


def _split_w_in(w_in, w_alpha, b_alpha):
    sizes = (GLA_DK, GLA_DK, GLA_DV, GLA_DV, GLA_GATE_RANK, MOBA_D, MOBA_D, MOBA_D, D_MODEL, D_MODEL)
    offs = [0]
    for s in sizes:
        offs.append(offs[-1] + s)
    parts = [w_in[:, offs[i]:offs[i + 1]].astype(BF16) for i in range(len(sizes))]
    wgq, wgk, wgv, wgr, wga, wmq, wmk, wmv, wa, wb = parts
    pad = LANES - GLA_GATE_RANK
    wga = jnp.pad(wga, ((0, 0), (0, pad)))
    wal = jnp.pad(w_alpha.astype(BF16), ((0, pad), (0, 0)))
    return (wgq, wgk, wgv, wgr, wga, wal, b_alpha.reshape(1, GLA_DK), wmq, wmk, wmv, wa, wb)


def kernel(x_prompt, x_sample, cache_k, cache_v, state_gla, page_table, ffn1_norm, ffn1_w_gate, ffn1_w_up, ffn1_w_down, mix_norm, w_in, gla_w_alpha, gla_b_alpha, gla_out_norm, w_o_gla, w_o_moba, w_out, ffn2_norm, ffn2_w_gate, ffn2_w_up, ffn2_w_down, final_norm):
    batch, t, _ = x_prompt.shape
    bs = x_sample.shape[0]
    depth = w_in.shape[0]
    past_len = page_table.shape[1] * cache_k.shape[2]
    tm_p, tm_in = 512, 256
    tm_s = bs

    xp = x_prompt.reshape(batch * t, D_MODEL)
    xs = x_sample.reshape(bs, D_MODEL)
    rope_p = _rope_tables(jnp.arange(t))
    rope_s = _rope_tables(jnp.full((tm_s,), past_len))
    tri_p = _chunk_tri(tm_in)
    tri_s = jnp.zeros((8, LANES), BF16)
    fn = final_norm.reshape(1, D_MODEL)
    row = lambda a: a.reshape(1, -1)

    kp_l, vp_l, sp_l, ks_l, vs_l, ss_l = [], [], [], [], [], []
    for l in range(depth):
        last = l == depth - 1
        f1 = (row(ffn1_norm[l]), ffn1_w_gate[l].astype(BF16), ffn1_w_up[l].astype(BF16),
              ffn1_w_down[l].astype(BF16))
        f2 = (row(ffn2_norm[l]), ffn2_w_gate[l].astype(BF16), ffn2_w_up[l].astype(BF16),
              ffn2_w_down[l].astype(BF16))
        w_proj = _split_w_in(w_in[l], gla_w_alpha[l], gla_b_alpha[l])
        w_o = (w_o_gla[l].astype(BF16), w_o_moba[l].astype(BF16), w_out[l].astype(BF16))
        gn = row(gla_out_norm[l])
        mixn = row(mix_norm[l])

        xp = _ffn(xp, *f1, fn, tm=tm_p, final_norm=False)
        gq, gk, gv, sgr, bl, mq, mk, mv, sa, sb = _inproj(
            xp, mixn, w_proj, rope_p, tri_p, tm=tm_in, n_pos_tiles=t // tm_in, cum=True)
        og, s_p = _gla_prompt(gq, gk, bl, gv, sgr, gn, batch=batch, t=t)
        om = _moba_prompt(mq, mk, mv, batch=batch, t=t)
        xp = _outproj(xp, og, om, sa, sb, *w_o, tm=tm_p)
        xp = _ffn(xp, *f2, fn, tm=tm_p, final_norm=last)
        kp_l.append(mk.reshape(batch, t, MOBA_HEADS, MOBA_HD))
        vp_l.append(mv.reshape(batch, t, MOBA_HEADS, MOBA_HD))
        sp_l.append(s_p)

        xs = _ffn(xs, *f1, fn, tm=tm_s, final_norm=False)
        gq, gk, gv, sgr, la, mq, mk, mv, sa, sb = _inproj(
            xs, mixn, w_proj, rope_s, tri_s, tm=tm_s, n_pos_tiles=1, cum=False)
        og, s_s = _gla_sample(gq, gk, la, gv, sgr, gn, state_gla[l])
        om = _moba_sample(mq, mk, mv, cache_k, cache_v, page_table, l)
        xs = _outproj(xs, og, om, sa, sb, *w_o, tm=tm_s)
        xs = _ffn(xs, *f2, fn, tm=tm_s, final_norm=last)
        ks_l.append(mk.reshape(bs, 1, MOBA_HEADS, MOBA_HD))
        vs_l.append(mv.reshape(bs, 1, MOBA_HEADS, MOBA_HD))
        ss_l.append(s_s)

    return (xp.reshape(batch, t, D_MODEL), xs.reshape(bs, 1, D_MODEL),
            jnp.stack(kp_l), jnp.stack(vp_l), jnp.stack(sp_l),
            jnp.stack(ks_l), jnp.stack(vs_l), jnp.stack(ss_l))
```

```python
import functools

import jax
import jax.numpy as jnp
from jax import lax
from jax.experimental import pallas as pl
from jax.experimental.pallas import tpu as pltpu

F32 = jnp.float32
BF16 = jnp.bfloat16

D_MODEL = 1024
GLA_HEADS = 4
GLA_DK = D_MODEL // 2
GLA_DV = D_MODEL
GLA_HDK = GLA_DK // GLA_HEADS
GLA_HDV = GLA_DV // GLA_HEADS
GLA_GATE_RANK = 16
GLA_GATE_TEMP = 16.0
GLA_LOGA_MIN = -4.0
MOBA_HEADS = 8
MOBA_HD = D_MODEL // MOBA_HEADS
MOBA_D = MOBA_HEADS * MOBA_HD
MOBA_BLOCK = 256
MOBA_TOPK = 3
ROT_DIM = MOBA_HD // 4
ROPE_THETA = 500000.0
D_FF = 2816
FFN_RES = 0.5
NORM_EPS = 1e-6

LANES = 128
FFN_CHUNK = 256
GLA_CHUNK = 32
GLA_MID = GLA_CHUNK // 2 - 1
GLA_UNROLL = 8
GLA_SAMPLE_ROWS = 4
MOBA_PAGES_PER_STEP = 8
LOG2E = 1.4426950408889634
NEG = -1e30
VMEM_LIMIT = 56 * 1024 * 1024

_NT = (((1,), (1,)), ((), ()))
_TN = (((0,), (0,)), ((), ()))


def _rms(x, g):
    return x * lax.rsqrt(jnp.mean(x * x, axis=-1, keepdims=True) + NORM_EPS) * g


def _dot(a, b):
    return jnp.dot(a, b, preferred_element_type=F32)


def _resident(shape):
    nd = len(shape)
    return pl.BlockSpec(shape, lambda *_: (0,) * nd, pipeline_mode=pl.Buffered(1))


def _params(*sem):
    return pltpu.CompilerParams(dimension_semantics=sem, vmem_limit_bytes=VMEM_LIMIT)


def _ffn_kernel(x_ref, g_ref, wg_ref, wu_ref, wd_ref, fn_ref, o_ref, a_scr, *, final_norm):
    x = x_ref[...]
    h = _rms(x, g_ref[...]).astype(BF16)
    for c in range(D_FF // FFN_CHUNK):
        sl = slice(c * FFN_CHUNK, (c + 1) * FFN_CHUNK)
        g = _dot(h, wg_ref[:, sl])
        u = _dot(h, wu_ref[:, sl])
        a_scr[:, sl] = (g * jax.nn.sigmoid(g) * u).astype(BF16)
    y = x + FFN_RES * _dot(a_scr[...], wd_ref[...])
    if final_norm:
        y = _rms(y, fn_ref[...])
    o_ref[...] = y


def _ffn(x, g, wg, wu, wd, fn, *, tm, final_norm):
    n = x.shape[0]
    row = pl.BlockSpec((tm, D_MODEL), lambda i: (i, 0))
    return pl.pallas_call(
        functools.partial(_ffn_kernel, final_norm=final_norm),
        grid=(n // tm,),
        in_specs=[row, _resident((1, D_MODEL)), _resident((D_MODEL, D_FF)),
                  _resident((D_MODEL, D_FF)), _resident((D_FF, D_MODEL)),
                  _resident((1, D_MODEL))],
        out_specs=row,
        out_shape=jax.ShapeDtypeStruct((n, D_MODEL), F32),
        scratch_shapes=[pltpu.VMEM((tm, D_FF), BF16)],
        compiler_params=_params("parallel"),
        name="ffn",
    )(x, g, wg, wu, wd, fn)


def _rope(y, ra, rb1, rb2):
    half = ROT_DIM // 2
    return (y * ra + pltpu.roll(y, LANES - half, 1) * rb1 + pltpu.roll(y, half, 1) * rb2)


def _inproj_kernel(x_ref, g_ref, wgq_ref, wgk_ref, wgv_ref, wgr_ref, wga_ref, wal_ref, bal_ref,
                   wmq_ref, wmk_ref, wmv_ref, wa_ref, wb_ref, ra_ref, rb1_ref, rb2_ref, tri_ref,
                   *rest, cum):
    gq_ref, gk_ref, gv_ref, sgr_ref, bl_ref, mq_ref, mk_ref, mv_ref, sa_ref, sb_ref = rest[-10:]
    h = _rms(x_ref[...], g_ref[...]).astype(BF16)
    gq_ref[...] = _dot(h, wgq_ref[...])
    gk_ref[...] = _dot(h, wgk_ref[...])
    gv_ref[...] = _dot(h, wgv_ref[...])
    gr = _dot(h, wgr_ref[...])
    sgr_ref[...] = gr * jax.nn.sigmoid(gr)
    ga = _dot(h, wga_ref[...]).astype(BF16)
    z = _dot(ga, wal_ref[...]) + bal_ref[...]
    log_a = (jnp.minimum(z, 0.0) - jnp.log(1.0 + jnp.exp(-jnp.abs(z)))) * (1.0 / GLA_GATE_TEMP)
    log_a = jnp.maximum(log_a, GLA_LOGA_MIN)
    if cum:
        hi = log_a.astype(BF16)
        lo = (log_a - hi.astype(F32)).astype(BF16)
        tri = tri_ref[...]
        bl_ref[...] = _dot(tri, hi) + _dot(tri, lo)
    else:
        bl_ref[...] = log_a
    ra, rb1, rb2 = ra_ref[...], rb1_ref[...], rb2_ref[...]
    mq = _dot(h, wmq_ref[...])
    mk = _dot(h, wmk_ref[...])
    for hd in range(MOBA_HEADS):
        sl = slice(hd * MOBA_HD, (hd + 1) * MOBA_HD)
        mq_ref[:, sl] = _rope(mq[:, sl], ra, rb1, rb2)
        mk_ref[:, sl] = _rope(mk[:, sl], ra, rb1, rb2)
    mv_ref[...] = _dot(h, wmv_ref[...])
    sa_ref[...] = jax.nn.sigmoid(_dot(h, wa_ref[...]))
    sb_ref[...] = jax.nn.sigmoid(_dot(h, wb_ref[...]))


def _inproj(x, g, w, rope, tri, *, tm, n_pos_tiles, cum, kv_stack=None, layer=0, depth=None):
    n = x.shape[0]
    row = lambda width: pl.BlockSpec((tm, width), lambda i: (i, 0))
    pos = pl.BlockSpec((tm, LANES), lambda i: (i % n_pos_tiles, 0))
    widths = (GLA_DK, GLA_DK, GLA_DV, GLA_DV, GLA_DK, MOBA_D, MOBA_D, MOBA_D, D_MODEL, D_MODEL)
    out_specs = [row(wd) for wd in widths]
    out_shape = [jax.ShapeDtypeStruct((n, wd), F32) for wd in widths]
    in_specs = ([row(D_MODEL), _resident((1, D_MODEL))] + [_resident(a.shape) for a in w]
                + [pos, pos, pos, _resident(tri.shape)])
    args = [x, g, *w, *rope, tri]
    aliases = {}
    if depth is not None:
        for o_idx in (6, 7):
            out_specs[o_idx] = pl.BlockSpec((None, tm, MOBA_D), lambda i: (layer, i, 0))
            out_shape[o_idx] = jax.ShapeDtypeStruct((depth, n, MOBA_D), F32)
        if kv_stack is not None:
            for o_idx, a in zip((6, 7), kv_stack):
                in_specs.append(pl.BlockSpec(memory_space=pl.ANY))
                args.append(a)
                aliases[len(args) - 1] = o_idx
    return pl.pallas_call(
        functools.partial(_inproj_kernel, cum=cum),
        grid=(n // tm,),
        in_specs=in_specs,
        out_specs=out_specs,
        out_shape=out_shape,
        input_output_aliases=aliases,
        compiler_params=_params("parallel"),
        name="inproj",
    )(*args)


def _gla_prompt_kernel(q_ref, k_ref, b_ref, v_ref, sgr_ref, gn_ref, o_ref, s_ref, st_scr):
    t = q_ref.shape[0]
    c_len = GLA_CHUNK
    st_scr[...] = jnp.zeros_like(st_scr)
    causal = (lax.broadcasted_iota(jnp.int32, (c_len, c_len), 0)
              >= lax.broadcasted_iota(jnp.int32, (c_len, c_len), 1))
    scale = GLA_HDK ** -0.5
    gn = gn_ref[...]

    def chunk(r0, st):
        rows = pl.ds(r0, c_len)
        q, k, b, v = q_ref[rows, :], k_ref[rows, :], b_ref[rows, :], v_ref[rows, :]
        b_mid = b[GLA_MID:GLA_MID + 1, :]
        b_last = b[c_len - 1:c_len, :]
        qt = q * jnp.exp(b - b_mid) * scale
        kt = k * jnp.exp(b_mid - b)
        att = lax.dot_general(qt.astype(BF16), kt.astype(BF16), _NT, preferred_element_type=F32)
        att = jnp.where(causal, att, 0.0)
        qs = (qt * jnp.exp(b_mid)).astype(BF16)
        ke = (kt * jnp.exp(b_last - b_mid)).astype(BF16)
        vb = v.astype(BF16)
        o = _dot(att.astype(BF16), vb) + lax.dot_general(
            qs, st.astype(BF16), _NT, preferred_element_type=F32)
        o_ref[rows, :] = _rms(o, gn) * sgr_ref[rows, :]
        return st * jnp.exp(b_last) + lax.dot_general(vb, ke, _TN, preferred_element_type=F32)

    def body(i, carry):
        st = st_scr[...]
        for u in range(GLA_UNROLL):
            st = chunk(pl.multiple_of((i * GLA_UNROLL + u) * c_len, c_len), st)
        st_scr[...] = st
        return carry

    lax.fori_loop(0, t // (c_len * GLA_UNROLL), body, 0)
    s_ref[0, 0] = st_scr[...].T


def _gla_prompt(gq, gk, bl, gv, sgr, gn, *, batch, t):
    n = gq.shape[0]
    kblk = pl.BlockSpec((t, GLA_HDK), lambda b, h: (b, h))
    vblk = pl.BlockSpec((t, GLA_HDV), lambda b, h: (b, h))
    return pl.pallas_call(
        _gla_prompt_kernel,
        grid=(batch, GLA_HEADS),
        in_specs=[kblk, kblk, kblk, vblk, vblk, pl.BlockSpec((1, GLA_HDV), lambda b, h: (0, 0))],
        out_specs=[vblk, pl.BlockSpec((1, 1, GLA_HDK, GLA_HDV), lambda b, h: (b, h, 0, 0))],
        out_shape=[jax.ShapeDtypeStruct((n, GLA_DV), F32),
                   jax.ShapeDtypeStruct((batch, GLA_HEADS, GLA_HDK, GLA_HDV), F32)],
        scratch_shapes=[pltpu.VMEM((GLA_HDV, GLA_HDK), F32)],
        compiler_params=_params("parallel", "parallel"),
        name="gla_prompt",
    )(gq, gk, bl, gv, sgr, gn)


def _column(r, width):
    col = jnp.broadcast_to(r, (LANES, LANES)).T
    return jnp.concatenate([col] * (width // LANES), axis=1)


def _gla_sample_kernel(q_ref, k_ref, la_ref, v_ref, sgr_ref, gn_ref, s_ref, *rest):
    o_ref, sn_ref = rest[-2:]
    scale = GLA_HDK ** -0.5
    gn = gn_ref[...]
    for r in range(q_ref.shape[1]):
        row = slice(r, r + 1)
        for h in range(GLA_HEADS):
            ks = slice(h * GLA_HDK, (h + 1) * GLA_HDK)
            vs = slice(h * GLA_HDV, (h + 1) * GLA_HDV)
            q, k, la = q_ref[0, row, ks], k_ref[0, row, ks], la_ref[0, row, ks]
            v = v_ref[0, row, vs]
            s_new = _column(jnp.exp(la), GLA_HDV) * s_ref[r, h] + _column(k, GLA_HDV) * v
            sn_ref[r, h] = s_new
            q8 = jnp.broadcast_to(q * scale, (8, GLA_HDK)).astype(BF16)
            o = _dot(q8, s_new.astype(BF16))[0:1, :]
            o_ref[0, row, vs] = _rms(o, gn) * sgr_ref[0, row, vs]


def _gla_sample(gq, gk, la, gv, sgr, gn, state_all, state_out, layer):
    bs = gq.shape[0]
    rows = GLA_SAMPLE_ROWS
    r3 = lambda a: a.reshape(bs // rows, rows, a.shape[-1])
    kblk = pl.BlockSpec((1, rows, GLA_DK), lambda b: (b, 0, 0))
    vblk = pl.BlockSpec((1, rows, GLA_DV), lambda b: (b, 0, 0))
    sblk = pl.BlockSpec((None, rows, GLA_HEADS, GLA_HDK, GLA_HDV), lambda b: (layer, b, 0, 0, 0))
    in_specs = [kblk, kblk, kblk, vblk, vblk, pl.BlockSpec((1, GLA_HDV), lambda b: (0, 0)), sblk]
    args = [r3(gq), r3(gk), r3(la), r3(gv), r3(sgr), gn, state_all]
    aliases = {}
    if state_out is not None:
        in_specs.append(pl.BlockSpec(memory_space=pl.ANY))
        args.append(state_out)
        aliases = {len(args) - 1: 1}
    o, s_new = pl.pallas_call(
        _gla_sample_kernel,
        grid=(bs // rows,),
        in_specs=in_specs,
        out_specs=[vblk, sblk],
        out_shape=[jax.ShapeDtypeStruct((bs // rows, rows, GLA_DV), F32),
                   jax.ShapeDtypeStruct(state_all.shape, F32)],
        input_output_aliases=aliases,
        compiler_params=_params("parallel"),
        name="gla_sample",
    )(*args)
    return o.reshape(bs, GLA_DV), s_new


def _rank_lt_topk(gate, n_valid, lane):
    rank = jnp.zeros_like(gate)
    for jp in range(n_valid):
        col = gate[:, jp:jp + 1]
        ge = jnp.where(col >= gate, 1.0, 0.0)
        gt = jnp.where(col > gate, 1.0, 0.0)
        rank = rank + jnp.where(lane > jp, ge, jnp.where(lane < jp, gt, 0.0))
    return jnp.where(rank < MOBA_TOPK, 1.0, 0.0)


def _moba_prompt_kernel(q_ref, k_ref, v_ref, o_ref):
    t = q_ref.shape[0]
    blk = MOBA_BLOCK
    nb = t // blk
    scale = MOBA_HD ** -0.5
    k = k_ref[...]
    kb = k.astype(BF16)
    vb = v_ref[...].astype(BF16)
    means = jnp.sum(k.reshape(nb, blk, MOBA_HD), axis=1) * (1.0 / blk)
    means = jnp.concatenate([means, jnp.zeros((LANES - nb, MOBA_HD), F32)], axis=0)
    causal = (lax.broadcasted_iota(jnp.int32, (blk, blk), 0)
              >= lax.broadcasted_iota(jnp.int32, (blk, blk), 1))
    lane = lax.broadcasted_iota(jnp.int32, (blk, LANES), 1)
    for i in range(nb):
        rows = slice(i * blk, (i + 1) * blk)
        q = q_ref[rows, :]
        width = (i + 1) * blk
        s = lax.dot_general((q * (scale * LOG2E)).astype(BF16), kb[:width], _NT,
                            preferred_element_type=F32)
        pieces = []
        if i > MOBA_TOPK:
            gate = lax.dot_general(q, means, _NT, preferred_element_type=F32,
                                   precision=lax.Precision.HIGHEST)
            sel = _rank_lt_topk(gate, i, lane)
            for j in range(i):
                bias = jnp.where(sel[:, j:j + 1] > 0.5, 0.0, NEG)
                pieces.append(s[:, j * blk:(j + 1) * blk] + bias)
        else:
            for j in range(i):
                pieces.append(s[:, j * blk:(j + 1) * blk])
        pieces.append(jnp.where(causal, s[:, i * blk:], NEG))
        sm = jnp.concatenate(pieces, axis=1) if len(pieces) > 1 else pieces[0]
        m = jnp.max(sm, axis=-1, keepdims=True)
        p = jnp.exp2(sm - m)
        l = jnp.sum(p, axis=-1, keepdims=True)
        o_ref[rows, :] = _dot(p.astype(BF16), vb[:width]) / l


def _moba_prompt(mq, mk_stack, mv_stack, layer, *, batch, t):
    n = mq.shape[0]
    blk = pl.BlockSpec((t, MOBA_HD), lambda b, h: (b, h))
    kv = pl.BlockSpec((None, t, MOBA_HD), lambda b, h: (layer, b, h))
    return pl.pallas_call(
        _moba_prompt_kernel,
        grid=(batch, MOBA_HEADS),
        in_specs=[blk, kv, kv],
        out_specs=blk,
        out_shape=jax.ShapeDtypeStruct((n, MOBA_D), F32),
        compiler_params=_params("parallel", "parallel"),
        name="moba_prompt",
    )(mq, mk_stack, mv_stack)


def _moba_sample_kernel(pt_ref, q_ref, kn_ref, vn_ref, *refs, pages_per_step, pages_per_block):
    del pt_ref
    kp_refs = refs[:pages_per_step]
    vp_refs = refs[pages_per_step:2 * pages_per_step]
    o_ref, m_scr, l_scr, o_scr, ks_scr = refs[2 * pages_per_step:]
    step = pl.program_id(1)
    shape = (MOBA_HEADS, MOBA_HD)
    q = q_ref[0]
    q2 = q * (MOBA_HD ** -0.5 * LOG2E)
    for i in range(pages_per_step):
        kp = kp_refs[i][...]
        s = jnp.sum(kp * q2[None], axis=-1, keepdims=True)
        m = jnp.max(s, axis=0)
        e = jnp.exp2(s - m[None])
        idx = step * pages_per_step + i
        m_scr[idx] = jnp.broadcast_to(m, shape)
        l_scr[idx] = jnp.broadcast_to(jnp.sum(e, axis=0), shape)
        o_scr[idx] = jnp.sum(e * vp_refs[i][...], axis=0)
        ks_scr[idx] = jnp.sum(kp, axis=0)

    @pl.when(step == pl.num_programs(1) - 1)
    def _():
        npg = m_scr.shape[0]
        nblk = npg // pages_per_block
        block_len = pages_per_block * kp_refs[0].shape[0]
        ksum = ks_scr[...].reshape(nblk, pages_per_block, *shape)
        means = jnp.sum(ksum, axis=1) * (1.0 / block_len)
        gate = jnp.broadcast_to(jnp.sum(means * q[None], axis=-1, keepdims=True), (nblk,) + shape)
        bidx = lax.broadcasted_iota(jnp.int32, (nblk,) + shape, 0)
        rank = jnp.zeros_like(gate)
        for jp in range(nblk):
            g = gate[jp][None]
            ge = jnp.where(g >= gate, 1.0, 0.0)
            gt = jnp.where(g > gate, 1.0, 0.0)
            rank = rank + jnp.where(bidx > jp, ge, jnp.where(bidx < jp, gt, 0.0))
        sel = jnp.where(rank < MOBA_TOPK, 1.0, 0.0)
        sel = jnp.broadcast_to(sel[:, None], (nblk, pages_per_block) + shape)
        sel = sel.reshape((npg,) + shape) > 0.5
        s_self = jnp.broadcast_to(jnp.sum(q2 * kn_ref[0], axis=-1, keepdims=True), shape)
        mp = jnp.where(sel, m_scr[...], NEG)
        m_all = jnp.maximum(jnp.max(mp, axis=0), s_self)
        w = jnp.where(sel, jnp.exp2(mp - m_all[None]), 0.0)
        w_self = jnp.exp2(s_self - m_all)
        l_all = w_self + jnp.sum(w * l_scr[...], axis=0)
        o_all = w_self * vn_ref[0] + jnp.sum(w * o_scr[...], axis=0)
        o_ref[0] = o_all / l_all


def _moba_sample(mq, mk, mv, cache_k, cache_v, page_table, layer):
    bs, n_pages = page_table.shape
    page = cache_k.shape[2]
    pps = MOBA_PAGES_PER_STEP
    r3 = lambda a: a.reshape(bs, MOBA_HEADS, MOBA_HD)
    tok = pl.BlockSpec((1, MOBA_HEADS, MOBA_HD), lambda b, p, pt: (b, 0, 0))

    def page_spec(i):
        return pl.BlockSpec((None, None, page, MOBA_HEADS, MOBA_HD),
                            lambda b, p, pt: (layer, pt[b * n_pages + p * pps + i], 0, 0, 0))

    pages = [page_spec(i) for i in range(pps)]
    stat = pltpu.VMEM((n_pages, MOBA_HEADS, MOBA_HD), F32)
    o = pl.pallas_call(
        functools.partial(_moba_sample_kernel, pages_per_step=pps,
                          pages_per_block=MOBA_BLOCK // page),
        grid_spec=pltpu.PrefetchScalarGridSpec(
            num_scalar_prefetch=1,
            grid=(bs, n_pages // pps),
            in_specs=[tok, tok, tok] + pages + pages,
            out_specs=tok,
            scratch_shapes=[stat, stat, stat, stat]),
        out_shape=jax.ShapeDtypeStruct((bs, MOBA_HEADS, MOBA_HD), F32),
        compiler_params=_params("parallel", "arbitrary"),
        name="moba_sample",
    )(page_table.reshape(-1), r3(mq), r3(mk), r3(mv), *([cache_k] * pps), *([cache_v] * pps))
    return o.reshape(bs, MOBA_D)


def _outproj_kernel(x_ref, og_ref, om_ref, sa_ref, sb_ref, wg_ref, wm_ref, wo_ref, o_ref):
    ya = _dot(og_ref[...].astype(BF16), wg_ref[...])
    yb = _dot(om_ref[...].astype(BF16), wm_ref[...])
    merged = sa_ref[...] * ya + sb_ref[...] * yb
    o_ref[...] = x_ref[...] + _dot(merged.astype(BF16), wo_ref[...])


def _outproj(x, og, om, sa, sb, wg, wm, wo, *, tm):
    n = x.shape[0]
    row = pl.BlockSpec((tm, D_MODEL), lambda i: (i, 0))
    wspec = _resident((D_MODEL, D_MODEL))
    return pl.pallas_call(
        _outproj_kernel,
        grid=(n // tm,),
        in_specs=[row] * 5 + [wspec] * 3,
        out_specs=row,
        out_shape=jax.ShapeDtypeStruct((n, D_MODEL), F32),
        compiler_params=_params("parallel"),
        name="outproj",
    )(x, og, om, sa, sb, wg, wm, wo)


def _rope_tables(pos):
    half = ROT_DIM // 2
    inv = ROPE_THETA ** (-jnp.arange(half, dtype=F32) * 2.0 / ROT_DIM)
    ang = pos.astype(F32)[:, None] * inv[None, :]
    cos, sin = jnp.cos(ang), jnp.sin(ang)
    n = pos.shape[0]
    zeros = lambda w: jnp.zeros((n, w), F32)
    ra = jnp.concatenate([cos, cos, jnp.ones((n, MOBA_HD - ROT_DIM), F32)], axis=1)
    rb1 = jnp.concatenate([-sin, zeros(MOBA_HD - half)], axis=1)
    rb2 = jnp.concatenate([zeros(half), sin, zeros(MOBA_HD - ROT_DIM)], axis=1)
    return ra, rb1, rb2


def _chunk_tri(tm):
    r = jnp.arange(tm)
    same = (r[:, None] // GLA_CHUNK) == (r[None, :] // GLA_CHUNK)
    return (same & (r[:, None] >= r[None, :])).astype(BF16)


def _split_w_in(w_in, w_alpha, b_alpha):
    sizes = (GLA_DK, GLA_DK, GLA_DV, GLA_DV, GLA_GATE_RANK, MOBA_D, MOBA_D, MOBA_D, D_MODEL, D_MODEL)
    offs = [0]
    for s in sizes:
        offs.append(offs[-1] + s)
    parts = [w_in[:, offs[i]:offs[i + 1]].astype(BF16) for i in range(len(sizes))]
    wgq, wgk, wgv, wgr, wga, wmq, wmk, wmv, wa, wb = parts
    pad = LANES - GLA_GATE_RANK
    wga = jnp.pad(wga, ((0, 0), (0, pad)))
    wal = jnp.pad(w_alpha.astype(BF16), ((0, pad), (0, 0)))
    return (wgq, wgk, wgv, wgr, wga, wal, b_alpha.reshape(1, GLA_DK), wmq, wmk, wmv, wa, wb)


def kernel(x_prompt, x_sample, cache_k, cache_v, state_gla, page_table, ffn1_norm, ffn1_w_gate, ffn1_w_up, ffn1_w_down, mix_norm, w_in, gla_w_alpha, gla_b_alpha, gla_out_norm, w_o_gla, w_o_moba, w_out, ffn2_norm, ffn2_w_gate, ffn2_w_up, ffn2_w_down, final_norm):
    batch, t, _ = x_prompt.shape
    bs = x_sample.shape[0]
    depth = w_in.shape[0]
    past_len = page_table.shape[1] * cache_k.shape[2]
    tm_p, tm_in = 512, 256
    tm_s = bs

    xp = x_prompt.reshape(batch * t, D_MODEL)
    xs = x_sample.reshape(bs, D_MODEL)
    rope_p = _rope_tables(jnp.arange(t))
    rope_s = _rope_tables(jnp.full((tm_s,), past_len))
    tri_p = _chunk_tri(tm_in)
    tri_s = jnp.zeros((8, LANES), BF16)
    fn = final_norm.reshape(1, D_MODEL)
    row = lambda a: a.reshape(1, -1)

    sp_l, ks_l, vs_l = [], [], []
    kv_stack, s_stack = None, None
    for l in range(depth):
        last = l == depth - 1
        f1 = (row(ffn1_norm[l]), ffn1_w_gate[l].astype(BF16), ffn1_w_up[l].astype(BF16),
              ffn1_w_down[l].astype(BF16))
        f2 = (row(ffn2_norm[l]), ffn2_w_gate[l].astype(BF16), ffn2_w_up[l].astype(BF16),
              ffn2_w_down[l].astype(BF16))
        w_proj = _split_w_in(w_in[l], gla_w_alpha[l], gla_b_alpha[l])
        w_o = (w_o_gla[l].astype(BF16), w_o_moba[l].astype(BF16), w_out[l].astype(BF16))
        gn = row(gla_out_norm[l])
        mixn = row(mix_norm[l])

        xp = _ffn(xp, *f1, fn, tm=tm_p, final_norm=False)
        gq, gk, gv, sgr, bl, mq, mk, mv, sa, sb = _inproj(
            xp, mixn, w_proj, rope_p, tri_p, tm=tm_in, n_pos_tiles=t // tm_in, cum=True,
            kv_stack=kv_stack, layer=l, depth=depth)
        kv_stack = (mk, mv)
        og, s_p = _gla_prompt(gq, gk, bl, gv, sgr, gn, batch=batch, t=t)
        om = _moba_prompt(mq, mk, mv, l, batch=batch, t=t)
        xp = _outproj(xp, og, om, sa, sb, *w_o, tm=tm_p)
        xp = _ffn(xp, *f2, fn, tm=tm_p, final_norm=last)
        sp_l.append(s_p)

        xs = _ffn(xs, *f1, fn, tm=tm_s, final_norm=False)
        gq, gk, gv, sgr, la, mq, mk, mv, sa, sb = _inproj(
            xs, mixn, w_proj, rope_s, tri_s, tm=tm_s, n_pos_tiles=1, cum=False)
        og, s_stack = _gla_sample(gq, gk, la, gv, sgr, gn, state_gla, s_stack, l)
        om = _moba_sample(mq, mk, mv, cache_k, cache_v, page_table, l)
        xs = _outproj(xs, og, om, sa, sb, *w_o, tm=tm_s)
        xs = _ffn(xs, *f2, fn, tm=tm_s, final_norm=last)
        ks_l.append(mk.reshape(bs, 1, MOBA_HEADS, MOBA_HD))
        vs_l.append(mv.reshape(bs, 1, MOBA_HEADS, MOBA_HD))

    kv_shape = (depth, batch, t, MOBA_HEADS, MOBA_HD)
    return (xp.reshape(batch, t, D_MODEL), xs.reshape(bs, 1, D_MODEL),
            kv_stack[0].reshape(kv_shape), kv_stack[1].reshape(kv_shape), jnp.stack(sp_l),
            jnp.stack(ks_l), jnp.stack(vs_l), s_stack)
```

```python
import functools

import jax
import jax.numpy as jnp
from jax import lax
from jax.experimental import pallas as pl
from jax.experimental.pallas import tpu as pltpu

F32 = jnp.float32
BF16 = jnp.bfloat16

D_MODEL = 1024
GLA_HEADS = 4
GLA_DK = D_MODEL // 2
GLA_DV = D_MODEL
GLA_HDK = GLA_DK // GLA_HEADS
GLA_HDV = GLA_DV // GLA_HEADS
GLA_GATE_RANK = 16
GLA_GATE_TEMP = 16.0
GLA_LOGA_MIN = -4.0
MOBA_HEADS = 8
MOBA_HD = D_MODEL // MOBA_HEADS
MOBA_D = MOBA_HEADS * MOBA_HD
MOBA_BLOCK = 256
MOBA_TOPK = 3
ROT_DIM = MOBA_HD // 4
ROPE_THETA = 500000.0
D_FF = 2816
FFN_RES = 0.5
NORM_EPS = 1e-6

LANES = 128
FFN_CHUNK = 256
GLA_CHUNK = 32
GLA_MID = GLA_CHUNK // 2 - 1
GLA_UNROLL = 8
GLA_SAMPLE_ROWS = 4
MOBA_PAGES_PER_STEP = 16
LOG2E = 1.4426950408889634
NEG = -1e30
VMEM_LIMIT = 56 * 1024 * 1024

_NT = (((1,), (1,)), ((), ()))
_TN = (((0,), (0,)), ((), ()))


def _rms(x, g):
    return x * lax.rsqrt(jnp.mean(x * x, axis=-1, keepdims=True) + NORM_EPS) * g


def _dot(a, b):
    return jnp.dot(a, b, preferred_element_type=F32)


def _resident(shape):
    nd = len(shape)
    return pl.BlockSpec(shape, lambda *_: (0,) * nd, pipeline_mode=pl.Buffered(1))


def _params(*sem):
    return pltpu.CompilerParams(dimension_semantics=sem, vmem_limit_bytes=VMEM_LIMIT)


def _ffn_kernel(x_ref, g_ref, wg_ref, wu_ref, wd_ref, fn_ref, o_ref, a_scr, *, final_norm):
    x = x_ref[...]
    h = _rms(x, g_ref[...]).astype(BF16)
    for c in range(D_FF // FFN_CHUNK):
        sl = slice(c * FFN_CHUNK, (c + 1) * FFN_CHUNK)
        g = _dot(h, wg_ref[:, sl])
        u = _dot(h, wu_ref[:, sl])
        a_scr[:, sl] = (g * jax.nn.sigmoid(g) * u).astype(BF16)
    y = x + FFN_RES * _dot(a_scr[...], wd_ref[...])
    if final_norm:
        y = _rms(y, fn_ref[...])
    o_ref[...] = y


def _ffn(x, g, wg, wu, wd, fn, *, tm, final_norm):
    n = x.shape[0]
    row = pl.BlockSpec((tm, D_MODEL), lambda i: (i, 0))
    return pl.pallas_call(
        functools.partial(_ffn_kernel, final_norm=final_norm),
        grid=(n // tm,),
        in_specs=[row, _resident((1, D_MODEL)), _resident((D_MODEL, D_FF)),
                  _resident((D_MODEL, D_FF)), _resident((D_FF, D_MODEL)),
                  _resident((1, D_MODEL))],
        out_specs=row,
        out_shape=jax.ShapeDtypeStruct((n, D_MODEL), F32),
        scratch_shapes=[pltpu.VMEM((tm, D_FF), BF16)],
        compiler_params=_params("parallel"),
        name="ffn",
    )(x, g, wg, wu, wd, fn)


def _rope(y, ra, rb1, rb2):
    half = ROT_DIM // 2
    return (y * ra + pltpu.roll(y, LANES - half, 1) * rb1 + pltpu.roll(y, half, 1) * rb2)


def _inproj_kernel(x_ref, g_ref, wgq_ref, wgk_ref, wgv_ref, wgr_ref, wga_ref, wal_ref, bal_ref,
                   wmq_ref, wmk_ref, wmv_ref, wa_ref, wb_ref, ra_ref, rb1_ref, rb2_ref, tri_ref,
                   *rest, cum):
    gq_ref, gk_ref, gv_ref, sgr_ref, bl_ref, mq_ref, mk_ref, mv_ref, sa_ref, sb_ref = rest[-10:]
    h = _rms(x_ref[...], g_ref[...]).astype(BF16)
    gq_ref[...] = _dot(h, wgq_ref[...])
    gk_ref[...] = _dot(h, wgk_ref[...])
    gv_ref[...] = _dot(h, wgv_ref[...])
    gr = _dot(h, wgr_ref[...])
    sgr_ref[...] = gr * jax.nn.sigmoid(gr)
    ga = _dot(h, wga_ref[...]).astype(BF16)
    z = _dot(ga, wal_ref[...]) + bal_ref[...]
    log_a = (jnp.minimum(z, 0.0) - jnp.log(1.0 + jnp.exp(-jnp.abs(z)))) * (1.0 / GLA_GATE_TEMP)
    log_a = jnp.maximum(log_a, GLA_LOGA_MIN)
    if cum:
        hi = log_a.astype(BF16)
        lo = (log_a - hi.astype(F32)).astype(BF16)
        tri = tri_ref[...]
        bl_ref[...] = _dot(tri, hi) + _dot(tri, lo)
    else:
        bl_ref[...] = log_a
    ra, rb1, rb2 = ra_ref[...], rb1_ref[...], rb2_ref[...]
    mq = _dot(h, wmq_ref[...])
    mk = _dot(h, wmk_ref[...])
    for hd in range(MOBA_HEADS):
        sl = slice(hd * MOBA_HD, (hd + 1) * MOBA_HD)
        mq_ref[:, sl] = _rope(mq[:, sl], ra, rb1, rb2)
        mk_ref[:, sl] = _rope(mk[:, sl], ra, rb1, rb2)
    mv_ref[...] = _dot(h, wmv_ref[...])
    sa_ref[...] = jax.nn.sigmoid(_dot(h, wa_ref[...]))
    sb_ref[...] = jax.nn.sigmoid(_dot(h, wb_ref[...]))


def _inproj(x, g, w, rope, tri, *, tm, n_pos_tiles, cum, kv_stack=None, layer=0, depth=None):
    n = x.shape[0]
    row = lambda width: pl.BlockSpec((tm, width), lambda i: (i, 0))
    pos = pl.BlockSpec((tm, LANES), lambda i: (i % n_pos_tiles, 0))
    widths = (GLA_DK, GLA_DK, GLA_DV, GLA_DV, GLA_DK, MOBA_D, MOBA_D, MOBA_D, D_MODEL, D_MODEL)
    out_specs = [row(wd) for wd in widths]
    out_shape = [jax.ShapeDtypeStruct((n, wd), F32) for wd in widths]
    in_specs = ([row(D_MODEL), _resident((1, D_MODEL))] + [_resident(a.shape) for a in w]
                + [pos, pos, pos, _resident(tri.shape)])
    args = [x, g, *w, *rope, tri]
    aliases = {}
    if depth is not None:
        for o_idx in (6, 7):
            out_specs[o_idx] = pl.BlockSpec((None, tm, MOBA_D), lambda i: (layer, i, 0))
            out_shape[o_idx] = jax.ShapeDtypeStruct((depth, n, MOBA_D), F32)
        if kv_stack is not None:
            for o_idx, a in zip((6, 7), kv_stack):
                in_specs.append(pl.BlockSpec(memory_space=pl.ANY))
                args.append(a)
                aliases[len(args) - 1] = o_idx
    return pl.pallas_call(
        functools.partial(_inproj_kernel, cum=cum),
        grid=(n // tm,),
        in_specs=in_specs,
        out_specs=out_specs,
        out_shape=out_shape,
        input_output_aliases=aliases,
        compiler_params=_params("parallel"),
        name="inproj",
    )(*args)


def _gla_prompt_kernel(q_ref, k_ref, b_ref, v_ref, sgr_ref, gn_ref, o_ref, s_ref, st_scr):
    t = q_ref.shape[0]
    c_len = GLA_CHUNK
    st_scr[...] = jnp.zeros_like(st_scr)
    causal = (lax.broadcasted_iota(jnp.int32, (c_len, c_len), 0)
              >= lax.broadcasted_iota(jnp.int32, (c_len, c_len), 1))
    scale = GLA_HDK ** -0.5
    gn = gn_ref[...]

    def decayed(rows):
        q, k, b = q_ref[rows, :], k_ref[rows, :], b_ref[rows, :]
        b_mid = b[GLA_MID:GLA_MID + 1, :]
        b_last = b[c_len - 1:c_len, :]
        qt = q * jnp.exp(b - b_mid) * scale
        kt = k * jnp.exp(b_mid - b)
        att = lax.dot_general(qt.astype(BF16), kt.astype(BF16), _NT, preferred_element_type=F32)
        att = jnp.where(causal, att, 0.0).astype(BF16)
        qs = (qt * jnp.exp(b_mid)).astype(BF16)
        ke = (kt * jnp.exp(b_last - b_mid)).astype(BF16)
        return att, qs, ke, jnp.exp(b_last)

    def local(rows, att, ke):
        vb = v_ref[rows, :].astype(BF16)
        return _dot(att, vb), lax.dot_general(vb, ke, _TN, preferred_element_type=F32)

    def body(i, carry):
        rows = [pl.ds(pl.multiple_of((i * GLA_UNROLL + u) * c_len, c_len), c_len)
                for u in range(GLA_UNROLL)]
        pre = [decayed(r) for r in rows]
        loc = [local(r, att, ke) for r, (att, _, ke, _) in zip(rows, pre)]
        st = st_scr[...]
        for r, (_, qs, _, d_last), (o_intra, kv) in zip(rows, pre, loc):
            o = o_intra + lax.dot_general(qs, st.astype(BF16), _NT, preferred_element_type=F32)
            o_ref[r, :] = _rms(o, gn) * sgr_ref[r, :]
            st = st * d_last + kv
        st_scr[...] = st
        return carry

    lax.fori_loop(0, t // (c_len * GLA_UNROLL), body, 0)
    s_ref[0, 0] = st_scr[...].T


def _gla_prompt(gq, gk, bl, gv, sgr, gn, *, batch, t):
    n = gq.shape[0]
    kblk = pl.BlockSpec((t, GLA_HDK), lambda b, h: (b, h))
    vblk = pl.BlockSpec((t, GLA_HDV), lambda b, h: (b, h))
    return pl.pallas_call(
        _gla_prompt_kernel,
        grid=(batch, GLA_HEADS),
        in_specs=[kblk, kblk, kblk, vblk, vblk, pl.BlockSpec((1, GLA_HDV), lambda b, h: (0, 0))],
        out_specs=[vblk, pl.BlockSpec((1, 1, GLA_HDK, GLA_HDV), lambda b, h: (b, h, 0, 0))],
        out_shape=[jax.ShapeDtypeStruct((n, GLA_DV), F32),
                   jax.ShapeDtypeStruct((batch, GLA_HEADS, GLA_HDK, GLA_HDV), F32)],
        scratch_shapes=[pltpu.VMEM((GLA_HDV, GLA_HDK), F32)],
        compiler_params=_params("parallel", "parallel"),
        name="gla_prompt",
    )(gq, gk, bl, gv, sgr, gn)


def _column(r, width):
    col = jnp.broadcast_to(r, (LANES, LANES)).T
    return jnp.concatenate([col] * (width // LANES), axis=1)


def _gla_sample_kernel(q_ref, k_ref, la_ref, v_ref, sgr_ref, gn_ref, s_ref, *rest):
    o_ref, sn_ref = rest[-2:]
    scale = GLA_HDK ** -0.5
    gn = gn_ref[...]
    for r in range(q_ref.shape[1]):
        row = slice(r, r + 1)
        for h in range(GLA_HEADS):
            ks = slice(h * GLA_HDK, (h + 1) * GLA_HDK)
            vs = slice(h * GLA_HDV, (h + 1) * GLA_HDV)
            q, k, la = q_ref[0, row, ks], k_ref[0, row, ks], la_ref[0, row, ks]
            v = v_ref[0, row, vs]
            s_new = _column(jnp.exp(la), GLA_HDV) * s_ref[r, h] + _column(k, GLA_HDV) * v
            sn_ref[r, h] = s_new
            q8 = jnp.broadcast_to(q * scale, (8, GLA_HDK)).astype(BF16)
            o = _dot(q8, s_new.astype(BF16))[0:1, :]
            o_ref[0, row, vs] = _rms(o, gn) * sgr_ref[0, row, vs]


def _gla_sample(gq, gk, la, gv, sgr, gn, state_all, state_out, layer):
    bs = gq.shape[0]
    rows = GLA_SAMPLE_ROWS
    r3 = lambda a: a.reshape(bs // rows, rows, a.shape[-1])
    kblk = pl.BlockSpec((1, rows, GLA_DK), lambda b: (b, 0, 0))
    vblk = pl.BlockSpec((1, rows, GLA_DV), lambda b: (b, 0, 0))
    sblk = pl.BlockSpec((None, rows, GLA_HEADS, GLA_HDK, GLA_HDV), lambda b: (layer, b, 0, 0, 0))
    in_specs = [kblk, kblk, kblk, vblk, vblk, pl.BlockSpec((1, GLA_HDV), lambda b: (0, 0)), sblk]
    args = [r3(gq), r3(gk), r3(la), r3(gv), r3(sgr), gn, state_all]
    aliases = {}
    if state_out is not None:
        in_specs.append(pl.BlockSpec(memory_space=pl.ANY))
        args.append(state_out)
        aliases = {len(args) - 1: 1}
    o, s_new = pl.pallas_call(
        _gla_sample_kernel,
        grid=(bs // rows,),
        in_specs=in_specs,
        out_specs=[vblk, sblk],
        out_shape=[jax.ShapeDtypeStruct((bs // rows, rows, GLA_DV), F32),
                   jax.ShapeDtypeStruct(state_all.shape, F32)],
        input_output_aliases=aliases,
        compiler_params=_params("parallel"),
        name="gla_sample",
    )(*args)
    return o.reshape(bs, GLA_DV), s_new


def _topk_mask(gate, n_valid, blk_idx):
    rank = jnp.zeros_like(gate)
    for jp in range(n_valid):
        row = gate[jp:jp + 1, :]
        ge = jnp.where(row >= gate, 1.0, 0.0)
        gt = jnp.where(row > gate, 1.0, 0.0)
        rank = rank + jnp.where(blk_idx > jp, ge, jnp.where(blk_idx < jp, gt, 0.0))
    return jnp.where(rank < MOBA_TOPK, 1.0, 0.0)


def _moba_prompt_kernel(q_ref, k_ref, v_ref, o_ref):
    t = q_ref.shape[0]
    blk = MOBA_BLOCK
    nb = t // blk
    scale = MOBA_HD ** -0.5
    k = k_ref[...]
    kb = k.astype(BF16)
    vt = v_ref[...].T.astype(BF16)
    means = jnp.sum(k.reshape(nb, blk, MOBA_HD), axis=1) * (1.0 / blk)
    causal_t = (lax.broadcasted_iota(jnp.int32, (blk, blk), 0)
                <= lax.broadcasted_iota(jnp.int32, (blk, blk), 1))
    blk_idx = lax.broadcasted_iota(jnp.int32, (nb, blk), 0)
    def masked_scores(i):
        q = q_ref[i * blk:(i + 1) * blk, :]
        qb = (q * (scale * LOG2E)).astype(BF16)
        s = lax.dot_general(kb[:(i + 1) * blk], qb, _NT, preferred_element_type=F32)
        pieces = []
        if i > MOBA_TOPK:
            gate = lax.dot_general(means, q, _NT, preferred_element_type=F32,
                                   precision=lax.Precision.HIGHEST)
            sel = _topk_mask(gate, i, blk_idx)
            for j in range(i):
                bias = jnp.where(sel[j:j + 1, :] > 0.5, 0.0, NEG)
                pieces.append(s[j * blk:(j + 1) * blk] + bias)
        else:
            for j in range(i):
                pieces.append(s[j * blk:(j + 1) * blk])
        pieces.append(jnp.where(causal_t, s[i * blk:], NEG))
        return jnp.concatenate(pieces, axis=0) if len(pieces) > 1 else pieces[0]

    def weighted_values(i, p, l):
        acc = _dot(vt[:, :(i + 1) * blk], p)
        o_ref[i * blk:(i + 1) * blk, :] = (acc / l).T

    sm_next = masked_scores(0)
    pending = None
    for i in range(nb):
        sm = sm_next
        if i + 1 < nb:
            sm_next = masked_scores(i + 1)
        m = jnp.max(sm, axis=0, keepdims=True)
        p = jnp.exp2(sm - m)
        l = jnp.sum(p, axis=0, keepdims=True)
        if pending is not None:
            weighted_values(*pending)
        pending = (i, p.astype(BF16), l)
    weighted_values(*pending)


def _moba_prompt(mq, mk_stack, mv_stack, layer, *, batch, t):
    n = mq.shape[0]
    blk = pl.BlockSpec((t, MOBA_HD), lambda b, h: (b, h))
    kv = pl.BlockSpec((None, t, MOBA_HD), lambda b, h: (layer, b, h))
    return pl.pallas_call(
        _moba_prompt_kernel,
        grid=(batch, MOBA_HEADS),
        in_specs=[blk, kv, kv],
        out_specs=blk,
        out_shape=jax.ShapeDtypeStruct((n, MOBA_D), F32),
        compiler_params=_params("parallel", "parallel"),
        name="moba_prompt",
    )(mq, mk_stack, mv_stack)


def _moba_sample_kernel(pt_ref, q_ref, kn_ref, vn_ref, *refs, pages_per_step, pages_per_block):
    del pt_ref
    kp_refs = refs[:pages_per_step]
    vp_refs = refs[pages_per_step:2 * pages_per_step]
    o_ref, m_scr, l_scr, o_scr, ks_scr = refs[2 * pages_per_step:]
    step = pl.program_id(1)
    shape = (MOBA_HEADS, MOBA_HD)
    q = q_ref[0]
    q2 = q * (MOBA_HD ** -0.5 * LOG2E)
    for i in range(pages_per_step):
        kp = kp_refs[i][...]
        s = jnp.sum(kp * q2[None], axis=-1, keepdims=True)
        m = jnp.max(s, axis=0)
        e = jnp.exp2(s - m[None])
        idx = step * pages_per_step + i
        m_scr[idx] = jnp.broadcast_to(m, shape)
        l_scr[idx] = jnp.broadcast_to(jnp.sum(e, axis=0), shape)
        o_scr[idx] = jnp.sum(e * vp_refs[i][...], axis=0)
        ks_scr[idx] = jnp.sum(kp, axis=0)

    @pl.when(step == pl.num_programs(1) - 1)
    def _():
        npg = m_scr.shape[0]
        nblk = npg // pages_per_block
        block_len = pages_per_block * kp_refs[0].shape[0]
        ksum = ks_scr[...].reshape(nblk, pages_per_block, *shape)
        means = jnp.sum(ksum, axis=1) * (1.0 / block_len)
        gate = jnp.broadcast_to(jnp.sum(means * q[None], axis=-1, keepdims=True), (nblk,) + shape)
        bidx = lax.broadcasted_iota(jnp.int32, (nblk,) + shape, 0)
        rank = jnp.zeros_like(gate)
        for jp in range(nblk):
            g = gate[jp][None]
            ge = jnp.where(g >= gate, 1.0, 0.0)
            gt = jnp.where(g > gate, 1.0, 0.0)
            rank = rank + jnp.where(bidx > jp, ge, jnp.where(bidx < jp, gt, 0.0))
        sel = jnp.where(rank < MOBA_TOPK, 1.0, 0.0)
        sel = jnp.broadcast_to(sel[:, None], (nblk, pages_per_block) + shape)
        sel = sel.reshape((npg,) + shape) > 0.5
        s_self = jnp.broadcast_to(jnp.sum(q2 * kn_ref[0], axis=-1, keepdims=True), shape)
        mp = jnp.where(sel, m_scr[...], NEG)
        m_all = jnp.maximum(jnp.max(mp, axis=0), s_self)
        w = jnp.where(sel, jnp.exp2(mp - m_all[None]), 0.0)
        w_self = jnp.exp2(s_self - m_all)
        l_all = w_self + jnp.sum(w * l_scr[...], axis=0)
        o_all = w_self * vn_ref[0] + jnp.sum(w * o_scr[...], axis=0)
        o_ref[0] = o_all / l_all


def _moba_sample(mq, mk, mv, cache_k, cache_v, page_table, layer):
    bs, n_pages = page_table.shape
    page = cache_k.shape[2]
    pps = MOBA_PAGES_PER_STEP
    r3 = lambda a: a.reshape(bs, MOBA_HEADS, MOBA_HD)
    tok = pl.BlockSpec((1, MOBA_HEADS, MOBA_HD), lambda b, p, pt: (b, 0, 0))

    def page_spec(i):
        return pl.BlockSpec((None, None, page, MOBA_HEADS, MOBA_HD),
                            lambda b, p, pt: (layer, pt[b * n_pages + p * pps + i], 0, 0, 0))

    pages = [page_spec(i) for i in range(pps)]
    stat = pltpu.VMEM((n_pages, MOBA_HEADS, MOBA_HD), F32)
    o = pl.pallas_call(
        functools.partial(_moba_sample_kernel, pages_per_step=pps,
                          pages_per_block=MOBA_BLOCK // page),
        grid_spec=pltpu.PrefetchScalarGridSpec(
            num_scalar_prefetch=1,
            grid=(bs, n_pages // pps),
            in_specs=[tok, tok, tok] + pages + pages,
            out_specs=tok,
            scratch_shapes=[stat, stat, stat, stat]),
        out_shape=jax.ShapeDtypeStruct((bs, MOBA_HEADS, MOBA_HD), F32),
        compiler_params=_params("parallel", "arbitrary"),
        name="moba_sample",
    )(page_table.reshape(-1), r3(mq), r3(mk), r3(mv), *([cache_k] * pps), *([cache_v] * pps))
    return o.reshape(bs, MOBA_D)


def _outproj_kernel(x_ref, og_ref, om_ref, sa_ref, sb_ref, wg_ref, wm_ref, wo_ref, o_ref):
    ya = _dot(og_ref[...].astype(BF16), wg_ref[...])
    yb = _dot(om_ref[...].astype(BF16), wm_ref[...])
    merged = sa_ref[...] * ya + sb_ref[...] * yb
    o_ref[...] = x_ref[...] + _dot(merged.astype(BF16), wo_ref[...])


def _outproj(x, og, om, sa, sb, wg, wm, wo, *, tm):
    n = x.shape[0]
    row = pl.BlockSpec((tm, D_MODEL), lambda i: (i, 0))
    wspec = _resident((D_MODEL, D_MODEL))
    return pl.pallas_call(
        _outproj_kernel,
        grid=(n // tm,),
        in_specs=[row] * 5 + [wspec] * 3,
        out_specs=row,
        out_shape=jax.ShapeDtypeStruct((n, D_MODEL), F32),
        compiler_params=_params("parallel"),
        name="outproj",
    )(x, og, om, sa, sb, wg, wm, wo)


def _rope_tables(pos):
    half = ROT_DIM // 2
    inv = ROPE_THETA ** (-jnp.arange(half, dtype=F32) * 2.0 / ROT_DIM)
    ang = pos.astype(F32)[:, None] * inv[None, :]
    cos, sin = jnp.cos(ang), jnp.sin(ang)
    n = pos.shape[0]
    zeros = lambda w: jnp.zeros((n, w), F32)
    ra = jnp.concatenate([cos, cos, jnp.ones((n, MOBA_HD - ROT_DIM), F32)], axis=1)
    rb1 = jnp.concatenate([-sin, zeros(MOBA_HD - half)], axis=1)
    rb2 = jnp.concatenate([zeros(half), sin, zeros(MOBA_HD - ROT_DIM)], axis=1)
    return ra, rb1, rb2


def _chunk_tri(tm):
    r = jnp.arange(tm)
    same = (r[:, None] // GLA_CHUNK) == (r[None, :] // GLA_CHUNK)
    return (same & (r[:, None] >= r[None, :])).astype(BF16)


def _split_w_in(w_in, w_alpha, b_alpha):
    sizes = (GLA_DK, GLA_DK, GLA_DV, GLA_DV, GLA_GATE_RANK, MOBA_D, MOBA_D, MOBA_D, D_MODEL, D_MODEL)
    offs = [0]
    for s in sizes:
        offs.append(offs[-1] + s)
    parts = [w_in[:, offs[i]:offs[i + 1]].astype(BF16) for i in range(len(sizes))]
    wgq, wgk, wgv, wgr, wga, wmq, wmk, wmv, wa, wb = parts
    pad = LANES - GLA_GATE_RANK
    wga = jnp.pad(wga, ((0, 0), (0, pad)))
    wal = jnp.pad(w_alpha.astype(BF16), ((0, pad), (0, 0)))
    return (wgq, wgk, wgv, wgr, wga, wal, b_alpha.reshape(1, GLA_DK), wmq, wmk, wmv, wa, wb)


def kernel(x_prompt, x_sample, cache_k, cache_v, state_gla, page_table, ffn1_norm, ffn1_w_gate, ffn1_w_up, ffn1_w_down, mix_norm, w_in, gla_w_alpha, gla_b_alpha, gla_out_norm, w_o_gla, w_o_moba, w_out, ffn2_norm, ffn2_w_gate, ffn2_w_up, ffn2_w_down, final_norm):
    batch, t, _ = x_prompt.shape
    bs = x_sample.shape[0]
    depth = w_in.shape[0]
    past_len = page_table.shape[1] * cache_k.shape[2]
    tm_p, tm_in = 512, 256
    tm_s = bs

    xp = x_prompt.reshape(batch * t, D_MODEL)
    xs = x_sample.reshape(bs, D_MODEL)
    rope_p = _rope_tables(jnp.arange(t))
    rope_s = _rope_tables(jnp.full((tm_s,), past_len))
    tri_p = _chunk_tri(tm_in)
    tri_s = jnp.zeros((8, LANES), BF16)
    fn = final_norm.reshape(1, D_MODEL)
    row = lambda a: a.reshape(1, -1)

    sp_l, ks_l, vs_l = [], [], []
    kv_stack, s_stack = None, None
    for l in range(depth):
        last = l == depth - 1
        f1 = (row(ffn1_norm[l]), ffn1_w_gate[l].astype(BF16), ffn1_w_up[l].astype(BF16),
              ffn1_w_down[l].astype(BF16))
        f2 = (row(ffn2_norm[l]), ffn2_w_gate[l].astype(BF16), ffn2_w_up[l].astype(BF16),
              ffn2_w_down[l].astype(BF16))
        w_proj = _split_w_in(w_in[l], gla_w_alpha[l], gla_b_alpha[l])
        w_o = (w_o_gla[l].astype(BF16), w_o_moba[l].astype(BF16), w_out[l].astype(BF16))
        gn = row(gla_out_norm[l])
        mixn = row(mix_norm[l])

        xp = _ffn(xp, *f1, fn, tm=tm_p, final_norm=False)
        gq, gk, gv, sgr, bl, mq, mk, mv, sa, sb = _inproj(
            xp, mixn, w_proj, rope_p, tri_p, tm=tm_in, n_pos_tiles=t // tm_in, cum=True,
            kv_stack=kv_stack, layer=l, depth=depth)
        kv_stack = (mk, mv)
        og, s_p = _gla_prompt(gq, gk, bl, gv, sgr, gn, batch=batch, t=t)
        om = _moba_prompt(mq, mk, mv, l, batch=batch, t=t)
        xp = _outproj(xp, og, om, sa, sb, *w_o, tm=tm_p)
        xp = _ffn(xp, *f2, fn, tm=tm_p, final_norm=last)
        sp_l.append(s_p)

        xs = _ffn(xs, *f1, fn, tm=tm_s, final_norm=False)
        gq, gk, gv, sgr, la, mq, mk, mv, sa, sb = _inproj(
            xs, mixn, w_proj, rope_s, tri_s, tm=tm_s, n_pos_tiles=1, cum=False)
        og, s_stack = _gla_sample(gq, gk, la, gv, sgr, gn, state_gla, s_stack, l)
        om = _moba_sample(mq, mk, mv, cache_k, cache_v, page_table, l)
        xs = _outproj(xs, og, om, sa, sb, *w_o, tm=tm_s)
        xs = _ffn(xs, *f2, fn, tm=tm_s, final_norm=last)
        ks_l.append(mk.reshape(bs, 1, MOBA_HEADS, MOBA_HD))
        vs_l.append(mv.reshape(bs, 1, MOBA_HEADS, MOBA_HD))

    kv_shape = (depth, batch, t, MOBA_HEADS, MOBA_HD)
    return (xp.reshape(batch, t, D_MODEL), xs.reshape(bs, 1, D_MODEL),
            kv_stack[0].reshape(kv_shape), kv_stack[1].reshape(kv_shape), jnp.stack(sp_l),
            jnp.stack(ks_l), jnp.stack(vs_l), s_stack)
```

```python
import functools

import jax
import jax.numpy as jnp
from jax import lax
from jax.experimental import pallas as pl
from jax.experimental.pallas import tpu as pltpu

F32 = jnp.float32
BF16 = jnp.bfloat16

D_MODEL = 1024
GLA_HEADS = 4
GLA_DK = D_MODEL // 2
GLA_DV = D_MODEL
GLA_HDK = GLA_DK // GLA_HEADS
GLA_HDV = GLA_DV // GLA_HEADS
GLA_GATE_RANK = 16
GLA_GATE_TEMP = 16.0
GLA_LOGA_MIN = -4.0
MOBA_HEADS = 8
MOBA_HD = D_MODEL // MOBA_HEADS
MOBA_D = MOBA_HEADS * MOBA_HD
MOBA_BLOCK = 256
MOBA_TOPK = 3
ROT_DIM = MOBA_HD // 4
ROPE_THETA = 500000.0
D_FF = 2816
FFN_RES = 0.5
NORM_EPS = 1e-6

LANES = 128
FFN_CHUNK = 256
GLA_CHUNK = 32
GLA_MID = GLA_CHUNK // 2 - 1
GLA_UNROLL = 16
GLA_SAMPLE_ROWS = 4
LOG2E = 1.4426950408889634
NEG = -1e30
VMEM_LIMIT = 56 * 1024 * 1024
VMEM_LIMIT_FUSED = 62 * 1024 * 1024

_NT = (((1,), (1,)), ((), ()))
_TN = (((0,), (0,)), ((), ()))


def _rms(x, g):
    return x * lax.rsqrt(jnp.mean(x * x, axis=-1, keepdims=True) + NORM_EPS) * g


def _dot(a, b):
    return jnp.dot(a, b, preferred_element_type=F32)


def _resident(shape):
    nd = len(shape)
    return pl.BlockSpec(shape, lambda *_: (0,) * nd, pipeline_mode=pl.Buffered(1))


def _params(*sem):
    return pltpu.CompilerParams(dimension_semantics=sem, vmem_limit_bytes=VMEM_LIMIT)


def _ffn_kernel(x_ref, g_ref, wg_ref, wu_ref, wd_ref, fn_ref, o_ref, a_scr, *, final_norm):
    x = x_ref[...]
    h = _rms(x, g_ref[...]).astype(BF16)
    for c in range(D_FF // FFN_CHUNK):
        sl = slice(c * FFN_CHUNK, (c + 1) * FFN_CHUNK)
        g = _dot(h, wg_ref[:, sl])
        u = _dot(h, wu_ref[:, sl])
        a_scr[:, sl] = (g * jax.nn.sigmoid(g) * u).astype(BF16)
    y = x + FFN_RES * _dot(a_scr[...], wd_ref[...])
    if final_norm:
        y = _rms(y, fn_ref[...])
    o_ref[...] = y


def _ffn(x, g, wg, wu, wd, fn, *, tm, final_norm):
    n = x.shape[0]
    row = pl.BlockSpec((tm, D_MODEL), lambda i: (i, 0))
    return pl.pallas_call(
        functools.partial(_ffn_kernel, final_norm=final_norm),
        grid=(n // tm,),
        in_specs=[row, _resident((1, D_MODEL)), _resident((D_MODEL, D_FF)),
                  _resident((D_MODEL, D_FF)), _resident((D_FF, D_MODEL)),
                  _resident((1, D_MODEL))],
        out_specs=row,
        out_shape=jax.ShapeDtypeStruct((n, D_MODEL), F32),
        scratch_shapes=[pltpu.VMEM((tm, D_FF), BF16)],
        compiler_params=_params("parallel"),
        name="ffn",
    )(x, g, wg, wu, wd, fn)


def _rope(y, ra, rb1, rb2):
    half = ROT_DIM // 2
    return (y * ra + pltpu.roll(y, LANES - half, 1) * rb1 + pltpu.roll(y, half, 1) * rb2)


def _inproj_kernel(x_ref, g_ref, wgq_ref, wgk_ref, wgv_ref, wgr_ref, wga_ref, wal_ref, bal_ref,
                   wmq_ref, wmk_ref, wmv_ref, wa_ref, wb_ref, ra_ref, rb1_ref, rb2_ref, tri_ref,
                   *rest, cum):
    gq_ref, gk_ref, gv_ref, sgr_ref, bl_ref, mq_ref, mk_ref, mv_ref, sa_ref, sb_ref = rest[-10:]
    h = _rms(x_ref[...], g_ref[...]).astype(BF16)
    gq_ref[...] = _dot(h, wgq_ref[...])
    gk_ref[...] = _dot(h, wgk_ref[...])
    gv_ref[...] = _dot(h, wgv_ref[...])
    gr = _dot(h, wgr_ref[...])
    sgr_ref[...] = gr * jax.nn.sigmoid(gr)
    ga = _dot(h, wga_ref[...]).astype(BF16)
    z = _dot(ga, wal_ref[...]) + bal_ref[...]
    log_a = (jnp.minimum(z, 0.0) - jnp.log(1.0 + jnp.exp(-jnp.abs(z)))) * (1.0 / GLA_GATE_TEMP)
    log_a = jnp.maximum(log_a, GLA_LOGA_MIN)
    if cum:
        hi = log_a.astype(BF16)
        lo = (log_a - hi.astype(F32)).astype(BF16)
        tri = tri_ref[...]
        bl_ref[...] = _dot(tri, hi) + _dot(tri, lo)
    else:
        bl_ref[...] = log_a
    ra, rb1, rb2 = ra_ref[...], rb1_ref[...], rb2_ref[...]
    mq = _dot(h, wmq_ref[...])
    mk = _dot(h, wmk_ref[...])
    for hd in range(MOBA_HEADS):
        sl = slice(hd * MOBA_HD, (hd + 1) * MOBA_HD)
        mq_ref[:, sl] = _rope(mq[:, sl], ra, rb1, rb2)
        mk_ref[:, sl] = _rope(mk[:, sl], ra, rb1, rb2)
    mv_ref[...] = _dot(h, wmv_ref[...])
    sa_ref[...] = jax.nn.sigmoid(_dot(h, wa_ref[...]))
    sb_ref[...] = jax.nn.sigmoid(_dot(h, wb_ref[...]))


def _inproj(x, g, w, rope, tri, *, tm, n_pos_tiles, cum, kv_stack=None, layer=0, depth=None):
    n = x.shape[0]
    row = lambda width: pl.BlockSpec((tm, width), lambda i: (i, 0))
    pos = pl.BlockSpec((tm, LANES), lambda i: (i % n_pos_tiles, 0))
    widths = (GLA_DK, GLA_DK, GLA_DV, GLA_DV, GLA_DK, MOBA_D, MOBA_D, MOBA_D, D_MODEL, D_MODEL)
    out_specs = [row(wd) for wd in widths]
    out_shape = [jax.ShapeDtypeStruct((n, wd), F32) for wd in widths]
    in_specs = ([row(D_MODEL), _resident((1, D_MODEL))] + [_resident(a.shape) for a in w]
                + [pos, pos, pos, _resident(tri.shape)])
    args = [x, g, *w, *rope, tri]
    aliases = {}
    if depth is not None:
        for o_idx in (6, 7):
            out_specs[o_idx] = pl.BlockSpec((None, tm, MOBA_D), lambda i: (layer, i, 0))
            out_shape[o_idx] = jax.ShapeDtypeStruct((depth, n, MOBA_D), F32)
        if kv_stack is not None:
            for o_idx, a in zip((6, 7), kv_stack):
                in_specs.append(pl.BlockSpec(memory_space=pl.ANY))
                args.append(a)
                aliases[len(args) - 1] = o_idx
    return pl.pallas_call(
        functools.partial(_inproj_kernel, cum=cum),
        grid=(n // tm,),
        in_specs=in_specs,
        out_specs=out_specs,
        out_shape=out_shape,
        input_output_aliases=aliases,
        compiler_params=_params("parallel"),
        name="inproj",
    )(*args)


def _gla_prompt_kernel(q_ref, k_ref, b_ref, v_ref, sgr_ref, gn_ref, o_ref, s_ref, st_scr):
    t = q_ref.shape[0]
    c_len = GLA_CHUNK
    st_scr[...] = jnp.zeros_like(st_scr)
    causal = (lax.broadcasted_iota(jnp.int32, (c_len, c_len), 0)
              >= lax.broadcasted_iota(jnp.int32, (c_len, c_len), 1))
    scale = GLA_HDK ** -0.5
    gn = gn_ref[...]

    def decayed(rows):
        q, k, b = q_ref[rows, :], k_ref[rows, :], b_ref[rows, :]
        b_mid = b[GLA_MID:GLA_MID + 1, :]
        b_last = b[c_len - 1:c_len, :]
        qt = q * jnp.exp(b - b_mid) * scale
        kt = k * jnp.exp(b_mid - b)
        att = lax.dot_general(qt.astype(BF16), kt.astype(BF16), _NT, preferred_element_type=F32)
        att = jnp.where(causal, att, 0.0).astype(BF16)
        qs = (qt * jnp.exp(b_mid)).astype(BF16)
        ke = (kt * jnp.exp(b_last - b_mid)).astype(BF16)
        return att, qs, ke, jnp.exp(b_last)

    def local(rows, att, ke):
        vb = v_ref[rows, :].astype(BF16)
        return _dot(att, vb), lax.dot_general(vb, ke, _TN, preferred_element_type=F32)

    def body(i, carry):
        rows = [pl.ds(pl.multiple_of((i * GLA_UNROLL + u) * c_len, c_len), c_len)
                for u in range(GLA_UNROLL)]
        pre = [decayed(r) for r in rows]
        loc = [local(r, att, ke) for r, (att, _, ke, _) in zip(rows, pre)]
        st = st_scr[...]
        for r, (_, qs, _, d_last), (o_intra, kv) in zip(rows, pre, loc):
            o = o_intra + lax.dot_general(qs, st.astype(BF16), _NT, preferred_element_type=F32)
            o_ref[r, :] = _rms(o, gn) * sgr_ref[r, :]
            st = st * d_last + kv
        st_scr[...] = st
        return carry

    lax.fori_loop(0, t // (c_len * GLA_UNROLL), body, 0)
    s_ref[0, 0] = st_scr[...].T


def _gla_prompt(gq, gk, bl, gv, sgr, gn, *, batch, t):
    n = gq.shape[0]
    kblk = pl.BlockSpec((t, GLA_HDK), lambda b, h: (b, h))
    vblk = pl.BlockSpec((t, GLA_HDV), lambda b, h: (b, h))
    return pl.pallas_call(
        _gla_prompt_kernel,
        grid=(batch, GLA_HEADS),
        in_specs=[kblk, kblk, kblk, vblk, vblk, pl.BlockSpec((1, GLA_HDV), lambda b, h: (0, 0))],
        out_specs=[vblk, pl.BlockSpec((1, 1, GLA_HDK, GLA_HDV), lambda b, h: (b, h, 0, 0))],
        out_shape=[jax.ShapeDtypeStruct((n, GLA_DV), F32),
                   jax.ShapeDtypeStruct((batch, GLA_HEADS, GLA_HDK, GLA_HDV), F32)],
        scratch_shapes=[pltpu.VMEM((GLA_HDV, GLA_HDK), F32)],
        compiler_params=_params("parallel", "parallel"),
        name="gla_prompt",
    )(gq, gk, bl, gv, sgr, gn)


def _column(r, width):
    col = jnp.broadcast_to(r, (LANES, LANES)).T
    return jnp.concatenate([col] * (width // LANES), axis=1)


def _gla_sample_kernel(q_ref, k_ref, la_ref, v_ref, sgr_ref, gn_ref, s_ref, *rest):
    o_ref, sn_ref = rest[-2:]
    scale = GLA_HDK ** -0.5
    gn = gn_ref[...]
    for r in range(q_ref.shape[1]):
        row = slice(r, r + 1)
        for h in range(GLA_HEADS):
            ks = slice(h * GLA_HDK, (h + 1) * GLA_HDK)
            vs = slice(h * GLA_HDV, (h + 1) * GLA_HDV)
            q, k, la = q_ref[0, row, ks], k_ref[0, row, ks], la_ref[0, row, ks]
            v = v_ref[0, row, vs]
            s_new = _column(jnp.exp(la), GLA_HDV) * s_ref[r, h] + _column(k, GLA_HDV) * v
            sn_ref[r, h] = s_new
            q8 = jnp.broadcast_to(q * scale, (8, GLA_HDK)).astype(BF16)
            o = _dot(q8, s_new.astype(BF16))[0:1, :]
            o_ref[0, row, vs] = _rms(o, gn) * sgr_ref[0, row, vs]


def _gla_sample(gq, gk, la, gv, sgr, gn, state_all, state_out, layer):
    bs = gq.shape[0]
    rows = GLA_SAMPLE_ROWS
    r3 = lambda a: a.reshape(bs // rows, rows, a.shape[-1])
    kblk = pl.BlockSpec((1, rows, GLA_DK), lambda b: (b, 0, 0))
    vblk = pl.BlockSpec((1, rows, GLA_DV), lambda b: (b, 0, 0))
    sblk = pl.BlockSpec((None, rows, GLA_HEADS, GLA_HDK, GLA_HDV), lambda b: (layer, b, 0, 0, 0))
    in_specs = [kblk, kblk, kblk, vblk, vblk, pl.BlockSpec((1, GLA_HDV), lambda b: (0, 0)), sblk]
    args = [r3(gq), r3(gk), r3(la), r3(gv), r3(sgr), gn, state_all]
    aliases = {}
    if state_out is not None:
        in_specs.append(pl.BlockSpec(memory_space=pl.ANY))
        args.append(state_out)
        aliases = {len(args) - 1: 1}
    o, s_new = pl.pallas_call(
        _gla_sample_kernel,
        grid=(bs // rows,),
        in_specs=in_specs,
        out_specs=[vblk, sblk],
        out_shape=[jax.ShapeDtypeStruct((bs // rows, rows, GLA_DV), F32),
                   jax.ShapeDtypeStruct(state_all.shape, F32)],
        input_output_aliases=aliases,
        compiler_params=_params("parallel"),
        name="gla_sample",
    )(*args)
    return o.reshape(bs, GLA_DV), s_new


def _topk_mask(gate, n_valid, blk_idx):
    rank = jnp.zeros_like(gate)
    for jp in range(n_valid):
        row = gate[jp:jp + 1, :]
        ge = jnp.where(row >= gate, 1.0, 0.0)
        gt = jnp.where(row > gate, 1.0, 0.0)
        rank = rank + jnp.where(blk_idx > jp, ge, jnp.where(blk_idx < jp, gt, 0.0))
    return jnp.where(rank < MOBA_TOPK, 1.0, 0.0)


def _moba_prompt_kernel(q_ref, k_ref, v_ref, o_ref):
    t = q_ref.shape[0]
    blk = MOBA_BLOCK
    nb = t // blk
    scale = MOBA_HD ** -0.5
    k = k_ref[...]
    kb = k.astype(BF16)
    vt = v_ref[...].T.astype(BF16)
    means = jnp.sum(k.reshape(nb, blk, MOBA_HD), axis=1) * (1.0 / blk)
    causal_t = (lax.broadcasted_iota(jnp.int32, (blk, blk), 0)
                <= lax.broadcasted_iota(jnp.int32, (blk, blk), 1))
    blk_idx = lax.broadcasted_iota(jnp.int32, (nb, blk), 0)
    def masked_scores(i):
        q = q_ref[i * blk:(i + 1) * blk, :]
        qb = (q * (scale * LOG2E)).astype(BF16)
        s = lax.dot_general(kb[:(i + 1) * blk], qb, _NT, preferred_element_type=F32)
        pieces = []
        if i > MOBA_TOPK:
            gate = lax.dot_general(means, q, _NT, preferred_element_type=F32,
                                   precision=lax.Precision.HIGHEST)
            sel = _topk_mask(gate, i, blk_idx)
            for j in range(i):
                bias = jnp.where(sel[j:j + 1, :] > 0.5, 0.0, NEG)
                pieces.append(s[j * blk:(j + 1) * blk] + bias)
        else:
            for j in range(i):
                pieces.append(s[j * blk:(j + 1) * blk])
        pieces.append(jnp.where(causal_t, s[i * blk:], NEG))
        return jnp.concatenate(pieces, axis=0) if len(pieces) > 1 else pieces[0]

    def weighted_values(i, p, l):
        acc = _dot(vt[:, :(i + 1) * blk], p)
        o_ref[i * blk:(i + 1) * blk, :] = (acc / l).T

    sm_next = masked_scores(0)
    pending = None
    for i in range(nb):
        sm = sm_next
        if i + 1 < nb:
            sm_next = masked_scores(i + 1)
        m = jnp.max(sm, axis=0, keepdims=True)
        p = jnp.exp2(sm - m)
        l = jnp.sum(p, axis=0, keepdims=True)
        if pending is not None:
            weighted_values(*pending)
        pending = (i, p.astype(BF16), l)
    weighted_values(*pending)


def _moba_prompt(mq, mk_stack, mv_stack, layer, *, batch, t):
    n = mq.shape[0]
    blk = pl.BlockSpec((t, MOBA_HD), lambda b, h: (b, h))
    kv = pl.BlockSpec((None, t, MOBA_HD), lambda b, h: (layer, b, h))
    return pl.pallas_call(
        _moba_prompt_kernel,
        grid=(batch, MOBA_HEADS),
        in_specs=[blk, kv, kv],
        out_specs=blk,
        out_shape=jax.ShapeDtypeStruct((n, MOBA_D), F32),
        compiler_params=_params("parallel", "parallel"),
        name="moba_prompt",
    )(mq, mk_stack, mv_stack)


_HEAD_TILE = (MOBA_HEADS, MOBA_HD)


def _page_partials(kp, vp, q2):
    s = jnp.sum(kp * q2[None], axis=-1, keepdims=True)
    m = jnp.max(s, axis=0)
    e = jnp.exp2(s - m[None])
    return (jnp.broadcast_to(m, _HEAD_TILE), jnp.broadcast_to(jnp.sum(e, axis=0), _HEAD_TILE),
            jnp.sum(e * vp, axis=0), jnp.sum(kp, axis=0))


def _moba_combine(q, q2, k_new, v_new, partials, pages_per_block, page_len):
    m_pg, l_pg, o_pg, ks_pg = (jnp.stack([p[i] for p in partials]) for i in range(4))
    npg = len(partials)
    nblk = npg // pages_per_block
    ksum = ks_pg.reshape(nblk, pages_per_block, *_HEAD_TILE)
    means = jnp.sum(ksum, axis=1) * (1.0 / (pages_per_block * page_len))
    gate = jnp.broadcast_to(jnp.sum(means * q[None], axis=-1, keepdims=True), (nblk,) + _HEAD_TILE)
    bidx = lax.broadcasted_iota(jnp.int32, (nblk,) + _HEAD_TILE, 0)
    rank = jnp.zeros_like(gate)
    for jp in range(nblk):
        g = gate[jp][None]
        ge = jnp.where(g >= gate, 1.0, 0.0)
        gt = jnp.where(g > gate, 1.0, 0.0)
        rank = rank + jnp.where(bidx > jp, ge, jnp.where(bidx < jp, gt, 0.0))
    sel = jnp.where(rank < MOBA_TOPK, 1.0, 0.0)
    sel = jnp.broadcast_to(sel[:, None], (nblk, pages_per_block) + _HEAD_TILE)
    sel = sel.reshape((npg,) + _HEAD_TILE) > 0.5
    s_self = jnp.broadcast_to(jnp.sum(q2 * k_new, axis=-1, keepdims=True), _HEAD_TILE)
    mp = jnp.where(sel, m_pg, NEG)
    m_all = jnp.maximum(jnp.max(mp, axis=0), s_self)
    w = jnp.where(sel, jnp.exp2(mp - m_all[None]), 0.0)
    w_self = jnp.exp2(s_self - m_all)
    l_all = w_self + jnp.sum(w * l_pg, axis=0)
    o_all = w_self * v_new + jnp.sum(w * o_pg, axis=0)
    return o_all / l_all


def _ffn_moba_kernel(pt_ref, x_ref, g_ref, wg_ref, wu_ref, wd_ref, fn_ref, q_ref, kn_ref, vn_ref,
                     *refs, n_pages, pages_per_block, final_norm):
    del pt_ref
    kp_refs = refs[:n_pages]
    vp_refs = refs[n_pages:2 * n_pages]
    y_ref, o_ref, a_scr = refs[2 * n_pages:]
    x = x_ref[...]
    h = _rms(x, g_ref[...]).astype(BF16)
    q = q_ref[0]
    q2 = q * (MOBA_HD ** -0.5 * LOG2E)
    n_chunks = D_FF // FFN_CHUNK
    partials = []
    for c in range(max(n_chunks, n_pages)):
        if c < n_chunks:
            sl = slice(c * FFN_CHUNK, (c + 1) * FFN_CHUNK)
            gate = _dot(h, wg_ref[:, sl])
            up = _dot(h, wu_ref[:, sl])
            a_scr[:, sl] = (gate * jax.nn.sigmoid(gate) * up).astype(BF16)
        if c < n_pages:
            partials.append(_page_partials(kp_refs[c][...], vp_refs[c][...], q2))
    y = x + FFN_RES * _dot(a_scr[...], wd_ref[...])
    if final_norm:
        y = _rms(y, fn_ref[...])
    y_ref[...] = y
    o_ref[0] = _moba_combine(q, q2, kn_ref[0], vn_ref[0], partials, pages_per_block,
                             kp_refs[0].shape[0])


def _ffn_moba(x, g, wg, wu, wd, fn, mq, mk, mv, cache_k, cache_v, page_table, layer, seq0, n_seq,
              *, final_norm):
    n = x.shape[0]
    bs, n_pages = page_table.shape
    page = cache_k.shape[2]
    assert n % n_seq == 0 and (n // n_seq) % 8 == 0
    tm = n // n_seq
    r3 = lambda a: a.reshape(bs, MOBA_HEADS, MOBA_HD)
    row = pl.BlockSpec((tm, D_MODEL), lambda i, pt: (i, 0))
    tok = pl.BlockSpec((1, MOBA_HEADS, MOBA_HD), lambda i, pt: (seq0 + i, 0, 0))

    def page_spec(p):
        return pl.BlockSpec((None, None, page, MOBA_HEADS, MOBA_HD),
                            lambda i, pt: (layer, pt[(seq0 + i) * n_pages + p], 0, 0, 0))

    pages = [page_spec(p) for p in range(n_pages)]
    y, o = pl.pallas_call(
        functools.partial(_ffn_moba_kernel, n_pages=n_pages, pages_per_block=MOBA_BLOCK // page,
                          final_norm=final_norm),
        grid_spec=pltpu.PrefetchScalarGridSpec(
            num_scalar_prefetch=1,
            grid=(n_seq,),
            in_specs=[row, _resident((1, D_MODEL)), _resident((D_MODEL, D_FF)),
                      _resident((D_MODEL, D_FF)), _resident((D_FF, D_MODEL)),
                      _resident((1, D_MODEL)), tok, tok, tok] + pages + pages,
            out_specs=[row, pl.BlockSpec((1, MOBA_HEADS, MOBA_HD), lambda i, pt: (i, 0, 0))],
            scratch_shapes=[pltpu.VMEM((tm, D_FF), BF16)]),
        out_shape=[jax.ShapeDtypeStruct((n, D_MODEL), F32),
                   jax.ShapeDtypeStruct((n_seq, MOBA_HEADS, MOBA_HD), F32)],
        compiler_params=pltpu.CompilerParams(dimension_semantics=("parallel",),
                                             vmem_limit_bytes=VMEM_LIMIT_FUSED),
        name="ffn_moba",
    )(page_table.reshape(-1), x, g, wg, wu, wd, fn, r3(mq), r3(mk), r3(mv),
      *([cache_k] * n_pages), *([cache_v] * n_pages))
    return y, o.reshape(n_seq, MOBA_D)


def _outproj_kernel(x_ref, og_ref, om_ref, sa_ref, sb_ref, wg_ref, wm_ref, wo_ref, o_ref):
    ya = _dot(og_ref[...].astype(BF16), wg_ref[...])
    yb = _dot(om_ref[...].astype(BF16), wm_ref[...])
    merged = sa_ref[...] * ya + sb_ref[...] * yb
    o_ref[...] = x_ref[...] + _dot(merged.astype(BF16), wo_ref[...])


def _outproj(x, og, om, sa, sb, wg, wm, wo, *, tm):
    n = x.shape[0]
    row = pl.BlockSpec((tm, D_MODEL), lambda i: (i, 0))
    wspec = _resident((D_MODEL, D_MODEL))
    return pl.pallas_call(
        _outproj_kernel,
        grid=(n // tm,),
        in_specs=[row] * 5 + [wspec] * 3,
        out_specs=row,
        out_shape=jax.ShapeDtypeStruct((n, D_MODEL), F32),
        compiler_params=_params("parallel"),
        name="outproj",
    )(x, og, om, sa, sb, wg, wm, wo)


def _rope_tables(pos):
    half = ROT_DIM // 2
    inv = ROPE_THETA ** (-jnp.arange(half, dtype=F32) * 2.0 / ROT_DIM)
    ang = pos.astype(F32)[:, None] * inv[None, :]
    cos, sin = jnp.cos(ang), jnp.sin(ang)
    n = pos.shape[0]
    zeros = lambda w: jnp.zeros((n, w), F32)
    ra = jnp.concatenate([cos, cos, jnp.ones((n, MOBA_HD - ROT_DIM), F32)], axis=1)
    rb1 = jnp.concatenate([-sin, zeros(MOBA_HD - half)], axis=1)
    rb2 = jnp.concatenate([zeros(half), sin, zeros(MOBA_HD - ROT_DIM)], axis=1)
    return ra, rb1, rb2


def _chunk_tri(tm):
    r = jnp.arange(tm)
    same = (r[:, None] // GLA_CHUNK) == (r[None, :] // GLA_CHUNK)
    return (same & (r[:, None] >= r[None, :])).astype(BF16)


def _split_w_in(w_in, w_alpha, b_alpha):
    sizes = (GLA_DK, GLA_DK, GLA_DV, GLA_DV, GLA_GATE_RANK, MOBA_D, MOBA_D, MOBA_D, D_MODEL, D_MODEL)
    offs = [0]
    for s in sizes:
        offs.append(offs[-1] + s)
    parts = [w_in[:, offs[i]:offs[i + 1]].astype(BF16) for i in range(len(sizes))]
    wgq, wgk, wgv, wgr, wga, wmq, wmk, wmv, wa, wb = parts
    pad = LANES - GLA_GATE_RANK
    wga = jnp.pad(wga, ((0, 0), (0, pad)))
    wal = jnp.pad(w_alpha.astype(BF16), ((0, pad), (0, 0)))
    return (wgq, wgk, wgv, wgr, wga, wal, b_alpha.reshape(1, GLA_DK), wmq, wmk, wmv, wa, wb)


def kernel(x_prompt, x_sample, cache_k, cache_v, state_gla, page_table, ffn1_norm, ffn1_w_gate, ffn1_w_up, ffn1_w_down, mix_norm, w_in, gla_w_alpha, gla_b_alpha, gla_out_norm, w_o_gla, w_o_moba, w_out, ffn2_norm, ffn2_w_gate, ffn2_w_up, ffn2_w_down, final_norm):
    batch, t, _ = x_prompt.shape
    bs = x_sample.shape[0]
    depth = w_in.shape[0]
    past_len = page_table.shape[1] * cache_k.shape[2]
    tm_p, tm_in = 512, 256
    tm_s = bs

    xp = x_prompt.reshape(batch * t, D_MODEL)
    xs = x_sample.reshape(bs, D_MODEL)
    rope_p = _rope_tables(jnp.arange(t))
    rope_s = _rope_tables(jnp.full((tm_s,), past_len))
    tri_p = _chunk_tri(tm_in)
    tri_s = jnp.zeros((8, LANES), BF16)
    fn = final_norm.reshape(1, D_MODEL)
    row = lambda a: a.reshape(1, -1)

    sp_l, ks_l, vs_l = [], [], []
    kv_stack, s_stack = None, None
    for l in range(depth):
        last = l == depth - 1
        f1 = (row(ffn1_norm[l]), ffn1_w_gate[l].astype(BF16), ffn1_w_up[l].astype(BF16),
              ffn1_w_down[l].astype(BF16))
        f2 = (row(ffn2_norm[l]), ffn2_w_gate[l].astype(BF16), ffn2_w_up[l].astype(BF16),
              ffn2_w_down[l].astype(BF16))
        w_proj = _split_w_in(w_in[l], gla_w_alpha[l], gla_b_alpha[l])
        w_o = (w_o_gla[l].astype(BF16), w_o_moba[l].astype(BF16), w_out[l].astype(BF16))
        gn = row(gla_out_norm[l])
        mixn = row(mix_norm[l])

        xs = _ffn(xs, *f1, fn, tm=tm_s, final_norm=False)
        gq_s, gk_s, gv_s, sgr_s, la_s, mq_s, mk_s, mv_s, sa_s, sb_s = _inproj(
            xs, mixn, w_proj, rope_s, tri_s, tm=tm_s, n_pos_tiles=1, cum=False)
        moba_s = (mq_s, mk_s, mv_s, cache_k, cache_v, page_table, l)
        half = bs // 2

        xp, om_s0 = _ffn_moba(xp, *f1, fn, *moba_s, 0, half, final_norm=False)
        gq, gk, gv, sgr, bl, mq, mk, mv, sa, sb = _inproj(
            xp, mixn, w_proj, rope_p, tri_p, tm=tm_in, n_pos_tiles=t // tm_in, cum=True,
            kv_stack=kv_stack, layer=l, depth=depth)
        kv_stack = (mk, mv)
        og, s_p = _gla_prompt(gq, gk, bl, gv, sgr, gn, batch=batch, t=t)
        om = _moba_prompt(mq, mk, mv, l, batch=batch, t=t)
        xp = _outproj(xp, og, om, sa, sb, *w_o, tm=tm_p)
        xp, om_s1 = _ffn_moba(xp, *f2, fn, *moba_s, half, bs - half, final_norm=last)
        sp_l.append(s_p)

        og_s, s_stack = _gla_sample(gq_s, gk_s, la_s, gv_s, sgr_s, gn, state_gla, s_stack, l)
        om_s = jnp.concatenate([om_s0, om_s1], axis=0)
        xs = _outproj(xs, og_s, om_s, sa_s, sb_s, *w_o, tm=tm_s)
        xs = _ffn(xs, *f2, fn, tm=tm_s, final_norm=last)
        ks_l.append(mk_s.reshape(bs, 1, MOBA_HEADS, MOBA_HD))
        vs_l.append(mv_s.reshape(bs, 1, MOBA_HEADS, MOBA_HD))

    kv_shape = (depth, batch, t, MOBA_HEADS, MOBA_HD)
    return (xp.reshape(batch, t, D_MODEL), xs.reshape(bs, 1, D_MODEL),
            kv_stack[0].reshape(kv_shape), kv_stack[1].reshape(kv_shape), jnp.stack(sp_l),
            jnp.stack(ks_l), jnp.stack(vs_l), s_stack)
```

```python
import functools

import jax
import jax.numpy as jnp
from jax import lax
from jax.experimental import pallas as pl
from jax.experimental.pallas import tpu as pltpu

F32 = jnp.float32
BF16 = jnp.bfloat16

D_MODEL = 1024
GLA_HEADS = 4
GLA_DK = D_MODEL // 2
GLA_DV = D_MODEL
GLA_HDK = GLA_DK // GLA_HEADS
GLA_HDV = GLA_DV // GLA_HEADS
GLA_GATE_RANK = 16
GLA_GATE_TEMP = 16.0
GLA_LOGA_MIN = -4.0
MOBA_HEADS = 8
MOBA_HD = D_MODEL // MOBA_HEADS
MOBA_D = MOBA_HEADS * MOBA_HD
MOBA_BLOCK = 256
MOBA_TOPK = 3
ROT_DIM = MOBA_HD // 4
ROPE_THETA = 500000.0
D_FF = 2816
FFN_RES = 0.5
NORM_EPS = 1e-6

LANES = 128
FFN_CHUNK = 256
GLA_CHUNK = 32
GLA_MID = GLA_CHUNK // 2 - 1
GLA_UNROLL = 16
GLA_SAMPLE_ROWS = 4
MOBA_SUBPAGES = 4
LOG2E = 1.4426950408889634
NEG = -1e30
VMEM_LIMIT = 56 * 1024 * 1024
VMEM_LIMIT_FUSED = 62 * 1024 * 1024

_NT = (((1,), (1,)), ((), ()))
_TN = (((0,), (0,)), ((), ()))


def _rms(x, g):
    return x * lax.rsqrt(jnp.mean(x * x, axis=-1, keepdims=True) + NORM_EPS) * g


def _dot(a, b):
    return jnp.dot(a, b, preferred_element_type=F32)


def _resident(shape):
    nd = len(shape)
    return pl.BlockSpec(shape, lambda *_: (0,) * nd, pipeline_mode=pl.Buffered(1))


def _layer_resident(stack, layer):
    return pl.BlockSpec((None,) + stack.shape[1:], lambda *_: (layer, 0, 0),
                        pipeline_mode=pl.Buffered(1))


def _params(*sem):
    return pltpu.CompilerParams(dimension_semantics=sem, vmem_limit_bytes=VMEM_LIMIT)


def _ffn_kernel(x_ref, g_ref, wg_ref, wu_ref, wd_ref, fn_ref, o_ref, a_scr, *, final_norm):
    x = x_ref[...]
    h = _rms(x, g_ref[...]).astype(BF16)
    for c in range(D_FF // FFN_CHUNK):
        sl = slice(c * FFN_CHUNK, (c + 1) * FFN_CHUNK)
        g = _dot(h, wg_ref[:, sl])
        u = _dot(h, wu_ref[:, sl])
        a_scr[:, sl] = (g * jax.nn.sigmoid(g) * u).astype(BF16)
    y = x + FFN_RES * _dot(a_scr[...], wd_ref[...])
    if final_norm:
        y = _rms(y, fn_ref[...])
    o_ref[...] = y


def _ffn(x, g, wg, wu, wd, fn, layer, *, tm, final_norm):
    n = x.shape[0]
    row = pl.BlockSpec((tm, D_MODEL), lambda i: (i, 0))
    return pl.pallas_call(
        functools.partial(_ffn_kernel, final_norm=final_norm),
        grid=(n // tm,),
        in_specs=[row] + [_layer_resident(a, layer) for a in (g, wg, wu, wd)]
        + [_resident((1, D_MODEL))],
        out_specs=row,
        out_shape=jax.ShapeDtypeStruct((n, D_MODEL), F32),
        scratch_shapes=[pltpu.VMEM((tm, D_FF), BF16)],
        compiler_params=_params("parallel"),
        name="ffn",
    )(x, g, wg, wu, wd, fn)


def _rope(y, ra, rb1, rb2):
    half = ROT_DIM // 2
    return (y * ra + pltpu.roll(y, LANES - half, 1) * rb1 + pltpu.roll(y, half, 1) * rb2)


def _inproj_kernel(x_ref, g_ref, wgq_ref, wgk_ref, wgv_ref, wgr_ref, wga_ref, wal_ref, bal_ref,
                   wmq_ref, wmk_ref, wmv_ref, wa_ref, wb_ref, ra_ref, rb1_ref, rb2_ref, tri_ref,
                   *rest, cum):
    gq_ref, gk_ref, gv_ref, sgr_ref, bl_ref, mq_ref, mk_ref, mv_ref, sa_ref, sb_ref = rest[-10:]
    h = _rms(x_ref[...], g_ref[...]).astype(BF16)
    gq_ref[...] = _dot(h, wgq_ref[...])
    gk_ref[...] = _dot(h, wgk_ref[...])
    gv_ref[...] = _dot(h, wgv_ref[...])
    gr = _dot(h, wgr_ref[...])
    sgr_ref[...] = gr * jax.nn.sigmoid(gr)
    ga = _dot(h, wga_ref[...]).astype(BF16)
    z = _dot(ga, wal_ref[...]) + bal_ref[...]
    log_a = (jnp.minimum(z, 0.0) - jnp.log(1.0 + jnp.exp(-jnp.abs(z)))) * (1.0 / GLA_GATE_TEMP)
    log_a = jnp.maximum(log_a, GLA_LOGA_MIN)
    if cum:
        hi = log_a.astype(BF16)
        lo = (log_a - hi.astype(F32)).astype(BF16)
        tri = tri_ref[...]
        bl_ref[...] = _dot(tri, hi) + _dot(tri, lo)
    else:
        bl_ref[...] = log_a
    ra, rb1, rb2 = ra_ref[...], rb1_ref[...], rb2_ref[...]
    mq = _dot(h, wmq_ref[...])
    mk = _dot(h, wmk_ref[...])
    for hd in range(MOBA_HEADS):
        sl = slice(hd * MOBA_HD, (hd + 1) * MOBA_HD)
        mq_ref[:, sl] = _rope(mq[:, sl], ra, rb1, rb2)
        mk_ref[:, sl] = _rope(mk[:, sl], ra, rb1, rb2)
    mv_ref[...] = _dot(h, wmv_ref[...])
    sa_ref[...] = jax.nn.sigmoid(_dot(h, wa_ref[...]))
    sb_ref[...] = jax.nn.sigmoid(_dot(h, wb_ref[...]))


def _inproj(x, g, w, rope, tri, *, tm, n_pos_tiles, cum, kv_stack=None, layer=0, depth=None):
    n = x.shape[0]
    row = lambda width: pl.BlockSpec((tm, width), lambda i: (i, 0))
    pos = pl.BlockSpec((tm, LANES), lambda i: (i % n_pos_tiles, 0))
    widths = (GLA_DK, GLA_DK, GLA_DV, GLA_DV, GLA_DK, MOBA_D, MOBA_D, MOBA_D, D_MODEL, D_MODEL)
    out_specs = [row(wd) for wd in widths]
    out_shape = [jax.ShapeDtypeStruct((n, wd), F32) for wd in widths]
    in_specs = ([row(D_MODEL)] + [_layer_resident(a, layer) for a in (g, *w)]
                + [pos, pos, pos, _resident(tri.shape)])
    args = [x, g, *w, *rope, tri]
    aliases = {}
    if depth is not None:
        for o_idx in (6, 7):
            out_specs[o_idx] = pl.BlockSpec((None, tm, MOBA_D), lambda i: (layer, i, 0))
            out_shape[o_idx] = jax.ShapeDtypeStruct((depth, n, MOBA_D), F32)
        if kv_stack is not None:
            for o_idx, a in zip((6, 7), kv_stack):
                in_specs.append(pl.BlockSpec(memory_space=pl.ANY))
                args.append(a)
                aliases[len(args) - 1] = o_idx
    return pl.pallas_call(
        functools.partial(_inproj_kernel, cum=cum),
        grid=(n // tm,),
        in_specs=in_specs,
        out_specs=out_specs,
        out_shape=out_shape,
        input_output_aliases=aliases,
        compiler_params=_params("parallel"),
        name="inproj",
    )(*args)


def _gla_prompt_kernel(q_ref, k_ref, b_ref, v_ref, sgr_ref, gn_ref, o_ref, s_ref, st_scr):
    t = q_ref.shape[0]
    c_len = GLA_CHUNK
    st_scr[...] = jnp.zeros_like(st_scr)
    causal = (lax.broadcasted_iota(jnp.int32, (c_len, c_len), 0)
              >= lax.broadcasted_iota(jnp.int32, (c_len, c_len), 1))
    scale = GLA_HDK ** -0.5
    gn = gn_ref[...]

    def decayed(rows):
        q, k, b = q_ref[rows, :], k_ref[rows, :], b_ref[rows, :]
        b_mid = b[GLA_MID:GLA_MID + 1, :]
        b_last = b[c_len - 1:c_len, :]
        qt = q * jnp.exp(b - b_mid) * scale
        kt = k * jnp.exp(b_mid - b)
        att = lax.dot_general(qt.astype(BF16), kt.astype(BF16), _NT, preferred_element_type=F32)
        att = jnp.where(causal, att, 0.0).astype(BF16)
        qs = (qt * jnp.exp(b_mid)).astype(BF16)
        ke = (kt * jnp.exp(b_last - b_mid)).astype(BF16)
        return att, qs, ke, jnp.exp(b_last)

    def local(rows, att, ke):
        vb = v_ref[rows, :].astype(BF16)
        return _dot(att, vb), lax.dot_general(vb, ke, _TN, preferred_element_type=F32)

    def body(i, carry):
        rows = [pl.ds(pl.multiple_of((i * GLA_UNROLL + u) * c_len, c_len), c_len)
                for u in range(GLA_UNROLL)]
        pre = [decayed(r) for r in rows]
        loc = [local(r, att, ke) for r, (att, _, ke, _) in zip(rows, pre)]
        st = st_scr[...]
        for r, (_, qs, _, d_last), (o_intra, kv) in zip(rows, pre, loc):
            o = o_intra + lax.dot_general(qs, st.astype(BF16), _NT, preferred_element_type=F32)
            o_ref[r, :] = _rms(o, gn) * sgr_ref[r, :]
            st = st * d_last + kv
        st_scr[...] = st
        return carry

    lax.fori_loop(0, t // (c_len * GLA_UNROLL), body, 0)
    s_ref[0, 0] = st_scr[...].T


def _gla_prompt(gq, gk, bl, gv, sgr, gn, *, batch, t):
    n = gq.shape[0]
    kblk = pl.BlockSpec((t, GLA_HDK), lambda b, h: (b, h))
    vblk = pl.BlockSpec((t, GLA_HDV), lambda b, h: (b, h))
    return pl.pallas_call(
        _gla_prompt_kernel,
        grid=(batch, GLA_HEADS),
        in_specs=[kblk, kblk, kblk, vblk, vblk, pl.BlockSpec((1, GLA_HDV), lambda b, h: (0, 0))],
        out_specs=[vblk, pl.BlockSpec((1, 1, GLA_HDK, GLA_HDV), lambda b, h: (b, h, 0, 0))],
        out_shape=[jax.ShapeDtypeStruct((n, GLA_DV), F32),
                   jax.ShapeDtypeStruct((batch, GLA_HEADS, GLA_HDK, GLA_HDV), F32)],
        scratch_shapes=[pltpu.VMEM((GLA_HDV, GLA_HDK), F32)],
        compiler_params=_params("parallel", "parallel"),
        name="gla_prompt",
    )(gq, gk, bl, gv, sgr, gn)


def _column(r, width):
    col = jnp.broadcast_to(r, (LANES, LANES)).T
    return jnp.concatenate([col] * (width // LANES), axis=1)


def _gla_sample_kernel(q_ref, k_ref, la_ref, v_ref, sgr_ref, gn_ref, s_ref, *rest):
    o_ref, sn_ref = rest[-2:]
    scale = GLA_HDK ** -0.5
    gn = gn_ref[...]
    for r in range(q_ref.shape[1]):
        row = slice(r, r + 1)
        for h in range(GLA_HEADS):
            ks = slice(h * GLA_HDK, (h + 1) * GLA_HDK)
            vs = slice(h * GLA_HDV, (h + 1) * GLA_HDV)
            q, k, la = q_ref[0, row, ks], k_ref[0, row, ks], la_ref[0, row, ks]
            v = v_ref[0, row, vs]
            s_new = _column(jnp.exp(la), GLA_HDV) * s_ref[r, h] + _column(k, GLA_HDV) * v
            sn_ref[r, h] = s_new
            q8 = jnp.broadcast_to(q * scale, (8, GLA_HDK)).astype(BF16)
            o = _dot(q8, s_new.astype(BF16))[0:1, :]
            o_ref[0, row, vs] = _rms(o, gn) * sgr_ref[0, row, vs]


def _gla_sample(gq, gk, la, gv, sgr, gn, state_all, state_out, layer):
    bs = gq.shape[0]
    rows = GLA_SAMPLE_ROWS
    r3 = lambda a: a.reshape(bs // rows, rows, a.shape[-1])
    kblk = pl.BlockSpec((1, rows, GLA_DK), lambda b: (b, 0, 0))
    vblk = pl.BlockSpec((1, rows, GLA_DV), lambda b: (b, 0, 0))
    sblk = pl.BlockSpec((None, rows, GLA_HEADS, GLA_HDK, GLA_HDV), lambda b: (layer, b, 0, 0, 0))
    in_specs = [kblk, kblk, kblk, vblk, vblk, pl.BlockSpec((1, GLA_HDV), lambda b: (0, 0)), sblk]
    args = [r3(gq), r3(gk), r3(la), r3(gv), r3(sgr), gn, state_all]
    aliases = {}
    if state_out is not None:
        in_specs.append(pl.BlockSpec(memory_space=pl.ANY))
        args.append(state_out)
        aliases = {len(args) - 1: 1}
    o, s_new = pl.pallas_call(
        _gla_sample_kernel,
        grid=(bs // rows,),
        in_specs=in_specs,
        out_specs=[vblk, sblk],
        out_shape=[jax.ShapeDtypeStruct((bs // rows, rows, GLA_DV), F32),
                   jax.ShapeDtypeStruct(state_all.shape, F32)],
        input_output_aliases=aliases,
        compiler_params=_params("parallel"),
        name="gla_sample",
    )(*args)
    return o.reshape(bs, GLA_DV), s_new


def _topk_mask(gate, n_valid, blk_idx):
    rank = jnp.zeros_like(gate)
    for jp in range(n_valid):
        row = gate[jp:jp + 1, :]
        ge = jnp.where(row >= gate, 1.0, 0.0)
        gt = jnp.where(row > gate, 1.0, 0.0)
        rank = rank + jnp.where(blk_idx > jp, ge, jnp.where(blk_idx < jp, gt, 0.0))
    return jnp.where(rank < MOBA_TOPK, 1.0, 0.0)


def _moba_prompt_kernel(q_ref, k_ref, v_ref, o_ref):
    t = q_ref.shape[0]
    blk = MOBA_BLOCK
    nb = t // blk
    scale = MOBA_HD ** -0.5
    k = k_ref[...]
    kb = k.astype(BF16)
    vt = v_ref[...].T.astype(BF16)
    means = jnp.sum(k.reshape(nb, blk, MOBA_HD), axis=1) * (1.0 / blk)
    causal_t = (lax.broadcasted_iota(jnp.int32, (blk, blk), 0)
                <= lax.broadcasted_iota(jnp.int32, (blk, blk), 1))
    blk_idx = lax.broadcasted_iota(jnp.int32, (nb, blk), 0)
    def masked_scores(i):
        q = q_ref[i * blk:(i + 1) * blk, :]
        qb = (q * (scale * LOG2E)).astype(BF16)
        s = lax.dot_general(kb[:(i + 1) * blk], qb, _NT, preferred_element_type=F32)
        pieces = []
        if i > MOBA_TOPK:
            gate = lax.dot_general(means, q, _NT, preferred_element_type=F32,
                                   precision=lax.Precision.HIGHEST)
            sel = _topk_mask(gate, i, blk_idx)
            for j in range(i):
                bias = jnp.where(sel[j:j + 1, :] > 0.5, 0.0, NEG)
                pieces.append(s[j * blk:(j + 1) * blk] + bias)
        else:
            for j in range(i):
                pieces.append(s[j * blk:(j + 1) * blk])
        pieces.append(jnp.where(causal_t, s[i * blk:], NEG))
        return jnp.concatenate(pieces, axis=0) if len(pieces) > 1 else pieces[0]

    def weighted_values(i, p, l):
        acc = _dot(vt[:, :(i + 1) * blk], p)
        o_ref[i * blk:(i + 1) * blk, :] = (acc / l).T

    sm_next = masked_scores(0)
    pending = None
    for i in range(nb):
        sm = sm_next
        if i + 1 < nb:
            sm_next = masked_scores(i + 1)
        m = jnp.max(sm, axis=0, keepdims=True)
        p = jnp.exp2(sm - m)
        l = jnp.sum(p, axis=0, keepdims=True)
        if pending is not None:
            weighted_values(*pending)
        pending = (i, p.astype(BF16), l)
    weighted_values(*pending)


def _moba_prompt(mq, mk_stack, mv_stack, layer, *, batch, t):
    n = mq.shape[0]
    blk = pl.BlockSpec((t, MOBA_HD), lambda b, h: (b, h))
    kv = pl.BlockSpec((None, t, MOBA_HD), lambda b, h: (layer, b, h))
    return pl.pallas_call(
        _moba_prompt_kernel,
        grid=(batch, MOBA_HEADS),
        in_specs=[blk, kv, kv],
        out_specs=blk,
        out_shape=jax.ShapeDtypeStruct((n, MOBA_D), F32),
        compiler_params=_params("parallel", "parallel"),
        name="moba_prompt",
    )(mq, mk_stack, mv_stack)


_HEAD_TILE = (MOBA_HEADS, MOBA_HD)


def _page_partials(kp, vp, q2):
    s = jnp.sum(kp * q2[None], axis=-1, keepdims=True)
    m = jnp.max(s, axis=0)
    e = jnp.exp2(s - m[None])
    return (jnp.broadcast_to(m, _HEAD_TILE), jnp.broadcast_to(jnp.sum(e, axis=0), _HEAD_TILE),
            jnp.sum(e * vp, axis=0), jnp.sum(kp, axis=0))


def _moba_combine(q, q2, k_new, v_new, partials, pages_per_block, page_len):
    m_pg, l_pg, o_pg, ks_pg = (jnp.stack([p[i] for p in partials]) for i in range(4))
    npg = len(partials)
    nblk = npg // pages_per_block
    ksum = ks_pg.reshape(nblk, pages_per_block, *_HEAD_TILE)
    means = jnp.sum(ksum, axis=1) * (1.0 / (pages_per_block * page_len))
    gate = jnp.broadcast_to(jnp.sum(means * q[None], axis=-1, keepdims=True), (nblk,) + _HEAD_TILE)
    bidx = lax.broadcasted_iota(jnp.int32, (nblk,) + _HEAD_TILE, 0)
    rank = jnp.zeros_like(gate)
    for jp in range(nblk):
        g = gate[jp][None]
        ge = jnp.where(g >= gate, 1.0, 0.0)
        gt = jnp.where(g > gate, 1.0, 0.0)
        rank = rank + jnp.where(bidx > jp, ge, jnp.where(bidx < jp, gt, 0.0))
    sel = jnp.where(rank < MOBA_TOPK, 1.0, 0.0)
    sel = jnp.broadcast_to(sel[:, None], (nblk, pages_per_block) + _HEAD_TILE)
    sel = sel.reshape((npg,) + _HEAD_TILE) > 0.5
    s_self = jnp.broadcast_to(jnp.sum(q2 * k_new, axis=-1, keepdims=True), _HEAD_TILE)
    mp = jnp.where(sel, m_pg, NEG)
    m_all = jnp.maximum(jnp.max(mp, axis=0), s_self)
    w = jnp.where(sel, jnp.exp2(mp - m_all[None]), 0.0)
    w_self = jnp.exp2(s_self - m_all)
    l_all = w_self + jnp.sum(w * l_pg, axis=0)
    o_all = w_self * v_new + jnp.sum(w * o_pg, axis=0)
    return o_all / l_all


def _ffn_moba_kernel(pt_ref, x_ref, g_ref, wg_ref, wu_ref, wd_ref, fn_ref, q_ref, kn_ref, vn_ref,
                     *refs, n_pages, pages_per_block, final_norm):
    del pt_ref
    kp_refs = refs[:n_pages]
    vp_refs = refs[n_pages:2 * n_pages]
    y_ref, o_ref, a_scr = refs[2 * n_pages:]
    x = x_ref[...]
    h = _rms(x, g_ref[...]).astype(BF16)
    q = q_ref[0]
    q2 = q * (MOBA_HD ** -0.5 * LOG2E)
    n_chunks = D_FF // FFN_CHUNK
    page_len = kp_refs[0].shape[0]
    sub = page_len // MOBA_SUBPAGES
    partials = []
    for c in range(max(n_chunks, n_pages)):
        if c < n_chunks:
            sl = slice(c * FFN_CHUNK, (c + 1) * FFN_CHUNK)
            gate = _dot(h, wg_ref[:, sl])
            up = _dot(h, wu_ref[:, sl])
            a_scr[:, sl] = (gate * jax.nn.sigmoid(gate) * up).astype(BF16)
        if c < n_pages:
            for r in range(0, page_len, sub):
                partials.append(_page_partials(kp_refs[c][r:r + sub], vp_refs[c][r:r + sub], q2))
    y = x + FFN_RES * _dot(a_scr[...], wd_ref[...])
    if final_norm:
        y = _rms(y, fn_ref[...])
    y_ref[...] = y
    o_ref[0] = _moba_combine(q, q2, kn_ref[0], vn_ref[0], partials,
                             pages_per_block * (page_len // sub), sub)


def _ffn_moba(x, g, wg, wu, wd, fn, mq, mk, mv, cache_k, cache_v, page_table, layer, seq0, n_seq,
              *, final_norm):
    n = x.shape[0]
    bs, n_pages = page_table.shape
    page = cache_k.shape[2]
    assert n % n_seq == 0 and (n // n_seq) % 8 == 0
    tm = n // n_seq
    r3 = lambda a: a.reshape(bs, MOBA_HEADS, MOBA_HD)
    row = pl.BlockSpec((tm, D_MODEL), lambda i, pt: (i, 0))
    tok = pl.BlockSpec((1, MOBA_HEADS, MOBA_HD), lambda i, pt: (seq0 + i, 0, 0))

    def page_spec(p):
        return pl.BlockSpec((None, None, page, MOBA_HEADS, MOBA_HD),
                            lambda i, pt: (layer, pt[(seq0 + i) * n_pages + p], 0, 0, 0))

    pages = [page_spec(p) for p in range(n_pages)]
    y, o = pl.pallas_call(
        functools.partial(_ffn_moba_kernel, n_pages=n_pages, pages_per_block=MOBA_BLOCK // page,
                          final_norm=final_norm),
        grid_spec=pltpu.PrefetchScalarGridSpec(
            num_scalar_prefetch=1,
            grid=(n_seq,),
            in_specs=[row] + [_layer_resident(a, layer) for a in (g, wg, wu, wd)]
            + [_resident((1, D_MODEL)), tok, tok, tok] + pages + pages,
            out_specs=[row, pl.BlockSpec((1, MOBA_HEADS, MOBA_HD), lambda i, pt: (i, 0, 0))],
            scratch_shapes=[pltpu.VMEM((tm, D_FF), BF16)]),
        out_shape=[jax.ShapeDtypeStruct((n, D_MODEL), F32),
                   jax.ShapeDtypeStruct((n_seq, MOBA_HEADS, MOBA_HD), F32)],
        compiler_params=pltpu.CompilerParams(dimension_semantics=("parallel",),
                                             vmem_limit_bytes=VMEM_LIMIT_FUSED),
        name="ffn_moba",
    )(page_table.reshape(-1), x, g, wg, wu, wd, fn, r3(mq), r3(mk), r3(mv),
      *([cache_k] * n_pages), *([cache_v] * n_pages))
    return y, o.reshape(n_seq, MOBA_D)


def _outproj_kernel(x_ref, og_ref, om_ref, sa_ref, sb_ref, wg_ref, wm_ref, wo_ref, o_ref):
    ya = _dot(og_ref[...].astype(BF16), wg_ref[...])
    yb = _dot(om_ref[...].astype(BF16), wm_ref[...])
    merged = sa_ref[...] * ya + sb_ref[...] * yb
    o_ref[...] = x_ref[...] + _dot(merged.astype(BF16), wo_ref[...])


def _outproj(x, og, om, sa, sb, wg, wm, wo, layer, *, tm):
    n = x.shape[0]
    row = pl.BlockSpec((tm, D_MODEL), lambda i: (i, 0))
    return pl.pallas_call(
        _outproj_kernel,
        grid=(n // tm,),
        in_specs=[row] * 5 + [_layer_resident(a, layer) for a in (wg, wm, wo)],
        out_specs=row,
        out_shape=jax.ShapeDtypeStruct((n, D_MODEL), F32),
        compiler_params=_params("parallel"),
        name="outproj",
    )(x, og, om, sa, sb, wg, wm, wo)


def _rope_tables(pos):
    half = ROT_DIM // 2
    inv = ROPE_THETA ** (-jnp.arange(half, dtype=F32) * 2.0 / ROT_DIM)
    ang = pos.astype(F32)[:, None] * inv[None, :]
    cos, sin = jnp.cos(ang), jnp.sin(ang)
    n = pos.shape[0]
    zeros = lambda w: jnp.zeros((n, w), F32)
    ra = jnp.concatenate([cos, cos, jnp.ones((n, MOBA_HD - ROT_DIM), F32)], axis=1)
    rb1 = jnp.concatenate([-sin, zeros(MOBA_HD - half)], axis=1)
    rb2 = jnp.concatenate([zeros(half), sin, zeros(MOBA_HD - ROT_DIM)], axis=1)
    return ra, rb1, rb2


def _chunk_tri(tm):
    r = jnp.arange(tm)
    same = (r[:, None] // GLA_CHUNK) == (r[None, :] // GLA_CHUNK)
    return (same & (r[:, None] >= r[None, :])).astype(BF16)


def _split_w_in(w_in, w_alpha, b_alpha):
    sizes = (GLA_DK, GLA_DK, GLA_DV, GLA_DV, GLA_GATE_RANK, MOBA_D, MOBA_D, MOBA_D, D_MODEL, D_MODEL)
    offs = [0]
    for s in sizes:
        offs.append(offs[-1] + s)
    parts = [w_in[:, :, offs[i]:offs[i + 1]].astype(BF16) for i in range(len(sizes))]
    wgq, wgk, wgv, wgr, wga, wmq, wmk, wmv, wa, wb = parts
    pad = LANES - GLA_GATE_RANK
    wga = jnp.pad(wga, ((0, 0), (0, 0), (0, pad)))
    wal = jnp.pad(w_alpha.astype(BF16), ((0, 0), (0, pad), (0, 0)))
    return (wgq, wgk, wgv, wgr, wga, wal, b_alpha[:, None, :], wmq, wmk, wmv, wa, wb)


def kernel(x_prompt, x_sample, cache_k, cache_v, state_gla, page_table, ffn1_norm, ffn1_w_gate, ffn1_w_up, ffn1_w_down, mix_norm, w_in, gla_w_alpha, gla_b_alpha, gla_out_norm, w_o_gla, w_o_moba, w_out, ffn2_norm, ffn2_w_gate, ffn2_w_up, ffn2_w_down, final_norm):
    batch, t, _ = x_prompt.shape
    bs = x_sample.shape[0]
    depth = w_in.shape[0]
    past_len = page_table.shape[1] * cache_k.shape[2]
    tm_p, tm_in = 512, 256
    tm_s = bs

    xp = x_prompt.reshape(batch * t, D_MODEL)
    xs = x_sample.reshape(bs, D_MODEL)
    rope_p = _rope_tables(jnp.arange(t))
    rope_s = _rope_tables(jnp.full((tm_s,), past_len))
    tri_p = _chunk_tri(tm_in)
    tri_s = jnp.zeros((8, LANES), BF16)
    fn = final_norm.reshape(1, D_MODEL)
    row = lambda a: a.reshape(1, -1)

    bf16 = lambda a: a.astype(BF16)
    f1 = (ffn1_norm[:, None, :], bf16(ffn1_w_gate), bf16(ffn1_w_up), bf16(ffn1_w_down))
    f2 = (ffn2_norm[:, None, :], bf16(ffn2_w_gate), bf16(ffn2_w_up), bf16(ffn2_w_down))
    w_proj = _split_w_in(w_in, gla_w_alpha, gla_b_alpha)
    w_o = (bf16(w_o_gla), bf16(w_o_moba), bf16(w_out))
    mixn = mix_norm[:, None, :]

    sp_l, ks_l, vs_l = [], [], []
    kv_stack, s_stack = None, None
    for l in range(depth):
        last = l == depth - 1
        gn = row(gla_out_norm[l])

        xs = _ffn(xs, *f1, fn, l, tm=tm_s, final_norm=False)
        gq_s, gk_s, gv_s, sgr_s, la_s, mq_s, mk_s, mv_s, sa_s, sb_s = _inproj(
            xs, mixn, w_proj, rope_s, tri_s, tm=tm_s, n_pos_tiles=1, cum=False, layer=l)
        moba_s = (mq_s, mk_s, mv_s, cache_k, cache_v, page_table, l)
        half = bs // 2

        xp, om_s0 = _ffn_moba(xp, *f1, fn, *moba_s, 0, half, final_norm=False)
        gq, gk, gv, sgr, bl, mq, mk, mv, sa, sb = _inproj(
            xp, mixn, w_proj, rope_p, tri_p, tm=tm_in, n_pos_tiles=t // tm_in, cum=True,
            kv_stack=kv_stack, layer=l, depth=depth)
        kv_stack = (mk, mv)
        og, s_p = _gla_prompt(gq, gk, bl, gv, sgr, gn, batch=batch, t=t)
        om = _moba_prompt(mq, mk, mv, l, batch=batch, t=t)
        xp = _outproj(xp, og, om, sa, sb, *w_o, l, tm=tm_p)
        xp, om_s1 = _ffn_moba(xp, *f2, fn, *moba_s, half, bs - half, final_norm=last)
        sp_l.append(s_p)

        og_s, s_stack = _gla_sample(gq_s, gk_s, la_s, gv_s, sgr_s, gn, state_gla, s_stack, l)
        om_s = jnp.concatenate([om_s0, om_s1], axis=0)
        xs = _outproj(xs, og_s, om_s, sa_s, sb_s, *w_o, l, tm=tm_s)
        xs = _ffn(xs, *f2, fn, l, tm=tm_s, final_norm=last)
        ks_l.append(mk_s.reshape(bs, 1, MOBA_HEADS, MOBA_HD))
        vs_l.append(mv_s.reshape(bs, 1, MOBA_HEADS, MOBA_HD))

    kv_shape = (depth, batch, t, MOBA_HEADS, MOBA_HD)
    return (xp.reshape(batch, t, D_MODEL), xs.reshape(bs, 1, D_MODEL),
            kv_stack[0].reshape(kv_shape), kv_stack[1].reshape(kv_shape), jnp.stack(sp_l),
            jnp.stack(ks_l), jnp.stack(vs_l), s_stack)
```

```python
import functools

import jax
import jax.numpy as jnp
from jax import lax
from jax.experimental import pallas as pl
from jax.experimental.pallas import tpu as pltpu

F32 = jnp.float32
BF16 = jnp.bfloat16

D_MODEL = 1024
GLA_HEADS = 4
GLA_DK = D_MODEL // 2
GLA_DV = D_MODEL
GLA_HDK = GLA_DK // GLA_HEADS
GLA_HDV = GLA_DV // GLA_HEADS
GLA_GATE_RANK = 16
GLA_GATE_TEMP = 16.0
GLA_LOGA_MIN = -4.0
MOBA_HEADS = 8
MOBA_HD = D_MODEL // MOBA_HEADS
MOBA_D = MOBA_HEADS * MOBA_HD
MOBA_BLOCK = 256
MOBA_TOPK = 3
ROT_DIM = MOBA_HD // 4
ROPE_THETA = 500000.0
D_FF = 2816
FFN_RES = 0.5
NORM_EPS = 1e-6

LANES = 128
FFN_CHUNK = 256
GLA_CHUNK = 32
GLA_MID = GLA_CHUNK // 2 - 1
GLA_UNROLL = 16
GLA_SAMPLE_ROWS = 8
MOBA_SUBPAGES = 4
LOG2E = 1.4426950408889634
NEG = -1e30
VMEM_LIMIT = 56 * 1024 * 1024
VMEM_LIMIT_FUSED = 62 * 1024 * 1024

_NT = (((1,), (1,)), ((), ()))
_TN = (((0,), (0,)), ((), ()))


def _rms(x, g):
    return x * lax.rsqrt(jnp.mean(x * x, axis=-1, keepdims=True) + NORM_EPS) * g


def _dot(a, b):
    return jnp.dot(a, b, preferred_element_type=F32)


def _resident(shape):
    nd = len(shape)
    return pl.BlockSpec(shape, lambda *_: (0,) * nd, pipeline_mode=pl.Buffered(1))


def _layer_resident(stack, layer):
    return pl.BlockSpec((None,) + stack.shape[1:], lambda *_: (layer, 0, 0),
                        pipeline_mode=pl.Buffered(1))


def _params(*sem):
    return pltpu.CompilerParams(dimension_semantics=sem, vmem_limit_bytes=VMEM_LIMIT)


def _ffn_kernel(x_ref, g_ref, wg_ref, wu_ref, wd_ref, fn_ref, o_ref, a_scr, *, final_norm):
    x = x_ref[...]
    h = _rms(x, g_ref[...]).astype(BF16)
    for c in range(D_FF // FFN_CHUNK):
        sl = slice(c * FFN_CHUNK, (c + 1) * FFN_CHUNK)
        g = _dot(h, wg_ref[:, sl])
        u = _dot(h, wu_ref[:, sl])
        a_scr[:, sl] = (g * jax.nn.sigmoid(g) * u).astype(BF16)
    y = x + FFN_RES * _dot(a_scr[...], wd_ref[...])
    if final_norm:
        y = _rms(y, fn_ref[...])
    o_ref[...] = y


def _ffn(x, g, wg, wu, wd, fn, layer, *, tm, final_norm):
    n = x.shape[0]
    row = pl.BlockSpec((tm, D_MODEL), lambda i: (i, 0))
    return pl.pallas_call(
        functools.partial(_ffn_kernel, final_norm=final_norm),
        grid=(n // tm,),
        in_specs=[row] + [_layer_resident(a, layer) for a in (g, wg, wu, wd)]
        + [_resident((1, D_MODEL))],
        out_specs=row,
        out_shape=jax.ShapeDtypeStruct((n, D_MODEL), F32),
        scratch_shapes=[pltpu.VMEM((tm, D_FF), BF16)],
        compiler_params=_params("parallel"),
        name="ffn",
    )(x, g, wg, wu, wd, fn)


def _rope(y, ra, rb1, rb2):
    half = ROT_DIM // 2
    return (y * ra + pltpu.roll(y, LANES - half, 1) * rb1 + pltpu.roll(y, half, 1) * rb2)


def _inproj_kernel(x_ref, g_ref, wgq_ref, wgk_ref, wgv_ref, wgr_ref, wga_ref, wal_ref, bal_ref,
                   wmq_ref, wmk_ref, wmv_ref, wa_ref, wb_ref, ra_ref, rb1_ref, rb2_ref, tri_ref,
                   *rest, cum):
    gq_ref, gk_ref, gv_ref, sgr_ref, bl_ref, mq_ref, mk_ref, mv_ref, sa_ref, sb_ref = rest[-10:]
    h = _rms(x_ref[...], g_ref[...]).astype(BF16)
    ga = _dot(h, wga_ref[...]).astype(BF16)
    z = _dot(ga, wal_ref[...]) + bal_ref[...]
    log_a = (jnp.minimum(z, 0.0) - jnp.log(1.0 + jnp.exp(-jnp.abs(z)))) * (1.0 / GLA_GATE_TEMP)
    log_a = jnp.maximum(log_a, GLA_LOGA_MIN)
    sa_ref[...] = jax.nn.sigmoid(_dot(h, wa_ref[...]))
    sb_ref[...] = jax.nn.sigmoid(_dot(h, wb_ref[...]))
    gr = _dot(h, wgr_ref[...])
    sgr_ref[...] = gr * jax.nn.sigmoid(gr)
    if cum:
        hi = log_a.astype(BF16)
        lo = (log_a - hi.astype(F32)).astype(BF16)
        tri = tri_ref[...]
        bl_ref[...] = _dot(tri, hi) + _dot(tri, lo)
    else:
        bl_ref[...] = log_a
    ra, rb1, rb2 = ra_ref[...], rb1_ref[...], rb2_ref[...]
    mq = _dot(h, wmq_ref[...])
    mk = _dot(h, wmk_ref[...])
    for hd in range(MOBA_HEADS):
        sl = slice(hd * MOBA_HD, (hd + 1) * MOBA_HD)
        mq_ref[:, sl] = _rope(mq[:, sl], ra, rb1, rb2)
        mk_ref[:, sl] = _rope(mk[:, sl], ra, rb1, rb2)
    gq_ref[...] = _dot(h, wgq_ref[...])
    gk_ref[...] = _dot(h, wgk_ref[...])
    gv_ref[...] = _dot(h, wgv_ref[...])
    mv_ref[...] = _dot(h, wmv_ref[...])


def _inproj(x, g, w, rope, tri, *, tm, n_pos_tiles, cum, kv_stack=None, layer=0, depth=None):
    n = x.shape[0]
    row = lambda width: pl.BlockSpec((tm, width), lambda i: (i, 0))
    pos = pl.BlockSpec((tm, LANES), lambda i: (i % n_pos_tiles, 0))
    widths = (GLA_DK, GLA_DK, GLA_DV, GLA_DV, GLA_DK, MOBA_D, MOBA_D, MOBA_D, D_MODEL, D_MODEL)
    out_specs = [row(wd) for wd in widths]
    out_shape = [jax.ShapeDtypeStruct((n, wd), F32) for wd in widths]
    in_specs = ([row(D_MODEL), _layer_resident(g, layer)] + [_resident(a.shape) for a in w]
                + [pos, pos, pos, _resident(tri.shape)])
    args = [x, g, *w, *rope, tri]
    aliases = {}
    if depth is not None:
        for o_idx in (6, 7):
            out_specs[o_idx] = pl.BlockSpec((None, tm, MOBA_D), lambda i: (layer, i, 0))
            out_shape[o_idx] = jax.ShapeDtypeStruct((depth, n, MOBA_D), F32)
        if kv_stack is not None:
            for o_idx, a in zip((6, 7), kv_stack):
                in_specs.append(pl.BlockSpec(memory_space=pl.ANY))
                args.append(a)
                aliases[len(args) - 1] = o_idx
    return pl.pallas_call(
        functools.partial(_inproj_kernel, cum=cum),
        grid=(n // tm,),
        in_specs=in_specs,
        out_specs=out_specs,
        out_shape=out_shape,
        input_output_aliases=aliases,
        compiler_params=_params("parallel"),
        name="inproj",
    )(*args)


def _gla_prompt_kernel(q_ref, k_ref, b_ref, v_ref, sgr_ref, gn_ref, o_ref, s_ref, st_scr):
    t = q_ref.shape[0]
    c_len = GLA_CHUNK
    st_scr[...] = jnp.zeros_like(st_scr)
    causal = (lax.broadcasted_iota(jnp.int32, (c_len, c_len), 0)
              >= lax.broadcasted_iota(jnp.int32, (c_len, c_len), 1))
    scale = GLA_HDK ** -0.5
    gn = gn_ref[...]

    def decayed(rows):
        q, k, b = q_ref[rows, :], k_ref[rows, :], b_ref[rows, :]
        b_mid = b[GLA_MID:GLA_MID + 1, :]
        b_last = b[c_len - 1:c_len, :]
        qt = q * jnp.exp(b - b_mid) * scale
        kt = k * jnp.exp(b_mid - b)
        att = lax.dot_general(qt.astype(BF16), kt.astype(BF16), _NT, preferred_element_type=F32)
        att = jnp.where(causal, att, 0.0).astype(BF16)
        qs = (qt * jnp.exp(b_mid)).astype(BF16)
        ke = (kt * jnp.exp(b_last - b_mid)).astype(BF16)
        return att, qs, ke, jnp.exp(b_last)

    def local(rows, att, ke):
        vb = v_ref[rows, :].astype(BF16)
        return _dot(att, vb), lax.dot_general(vb, ke, _TN, preferred_element_type=F32)

    def body(i, carry):
        rows = [pl.ds(pl.multiple_of((i * GLA_UNROLL + u) * c_len, c_len), c_len)
                for u in range(GLA_UNROLL)]
        pre = [decayed(r) for r in rows]
        loc = [local(r, att, ke) for r, (att, _, ke, _) in zip(rows, pre)]
        st = st_scr[...]
        for r, (_, qs, _, d_last), (o_intra, kv) in zip(rows, pre, loc):
            o = o_intra + lax.dot_general(qs, st.astype(BF16), _NT, preferred_element_type=F32)
            o_ref[r, :] = _rms(o, gn) * sgr_ref[r, :]
            st = st * d_last + kv
        st_scr[...] = st
        return carry

    lax.fori_loop(0, t // (c_len * GLA_UNROLL), body, 0)
    s_ref[0, 0] = st_scr[...].T


def _gla_prompt(gq, gk, bl, gv, sgr, gn, *, batch, t):
    n = gq.shape[0]
    kblk = pl.BlockSpec((t, GLA_HDK), lambda b, h: (b, h))
    vblk = pl.BlockSpec((t, GLA_HDV), lambda b, h: (b, h))
    return pl.pallas_call(
        _gla_prompt_kernel,
        grid=(batch, GLA_HEADS),
        in_specs=[kblk, kblk, kblk, vblk, vblk, pl.BlockSpec((1, GLA_HDV), lambda b, h: (0, 0))],
        out_specs=[vblk, pl.BlockSpec((1, 1, GLA_HDK, GLA_HDV), lambda b, h: (b, h, 0, 0))],
        out_shape=[jax.ShapeDtypeStruct((n, GLA_DV), F32),
                   jax.ShapeDtypeStruct((batch, GLA_HEADS, GLA_HDK, GLA_HDV), F32)],
        scratch_shapes=[pltpu.VMEM((GLA_HDV, GLA_HDK), F32)],
        compiler_params=_params("parallel", "parallel"),
        name="gla_prompt",
    )(gq, gk, bl, gv, sgr, gn)


def _column(r, width):
    col = jnp.broadcast_to(r, (LANES, LANES)).T
    return jnp.concatenate([col] * (width // LANES), axis=1)


def _gla_sample_kernel(q_ref, k_ref, la_ref, v_ref, sgr_ref, gn_ref, s_ref, *rest):
    o_ref, sn_ref = rest[-2:]
    scale = GLA_HDK ** -0.5
    gn = gn_ref[...]
    for r in range(q_ref.shape[1]):
        row = slice(r, r + 1)
        for h in range(GLA_HEADS):
            ks = slice(h * GLA_HDK, (h + 1) * GLA_HDK)
            vs = slice(h * GLA_HDV, (h + 1) * GLA_HDV)
            q, k, la = q_ref[0, row, ks], k_ref[0, row, ks], la_ref[0, row, ks]
            v = v_ref[0, row, vs]
            s_new = _column(jnp.exp(la), GLA_HDV) * s_ref[r, h] + _column(k, GLA_HDV) * v
            sn_ref[r, h] = s_new
            q8 = jnp.broadcast_to(q * scale, (8, GLA_HDK)).astype(BF16)
            o = _dot(q8, s_new.astype(BF16))[0:1, :]
            o_ref[0, row, vs] = _rms(o, gn) * sgr_ref[0, row, vs]


def _gla_sample(gq, gk, la, gv, sgr, gn, state_all, state_out, layer):
    bs = gq.shape[0]
    rows = GLA_SAMPLE_ROWS
    r3 = lambda a: a.reshape(bs // rows, rows, a.shape[-1])
    kblk = pl.BlockSpec((1, rows, GLA_DK), lambda b: (b, 0, 0))
    vblk = pl.BlockSpec((1, rows, GLA_DV), lambda b: (b, 0, 0))
    sblk = pl.BlockSpec((None, rows, GLA_HEADS, GLA_HDK, GLA_HDV), lambda b: (layer, b, 0, 0, 0))
    in_specs = [kblk, kblk, kblk, vblk, vblk, pl.BlockSpec((1, GLA_HDV), lambda b: (0, 0)), sblk]
    args = [r3(gq), r3(gk), r3(la), r3(gv), r3(sgr), gn, state_all]
    aliases = {}
    if state_out is not None:
        in_specs.append(pl.BlockSpec(memory_space=pl.ANY))
        args.append(state_out)
        aliases = {len(args) - 1: 1}
    o, s_new = pl.pallas_call(
        _gla_sample_kernel,
        grid=(bs // rows,),
        in_specs=in_specs,
        out_specs=[vblk, sblk],
        out_shape=[jax.ShapeDtypeStruct((bs // rows, rows, GLA_DV), F32),
                   jax.ShapeDtypeStruct(state_all.shape, F32)],
        input_output_aliases=aliases,
        compiler_params=_params("parallel"),
        name="gla_sample",
    )(*args)
    return o.reshape(bs, GLA_DV), s_new


def _topk_mask(gate, n_valid, blk_idx):
    rank = jnp.zeros_like(gate)
    for jp in range(n_valid):
        row = gate[jp:jp + 1, :]
        ge = jnp.where(row >= gate, 1.0, 0.0)
        gt = jnp.where(row > gate, 1.0, 0.0)
        rank = rank + jnp.where(blk_idx > jp, ge, jnp.where(blk_idx < jp, gt, 0.0))
    return jnp.where(rank < MOBA_TOPK, 1.0, 0.0)


def _moba_prompt_kernel(q_ref, k_ref, v_ref, o_ref):
    t = q_ref.shape[0]
    blk = MOBA_BLOCK
    nb = t // blk
    scale = MOBA_HD ** -0.5
    k = k_ref[...]
    kb = k.astype(BF16)
    vt = v_ref[...].T.astype(BF16)
    means = jnp.sum(k.reshape(nb, blk, MOBA_HD), axis=1) * (1.0 / blk)
    causal_t = (lax.broadcasted_iota(jnp.int32, (blk, blk), 0)
                <= lax.broadcasted_iota(jnp.int32, (blk, blk), 1))
    blk_idx = lax.broadcasted_iota(jnp.int32, (nb, blk), 0)
    def masked_scores(i):
        q = q_ref[i * blk:(i + 1) * blk, :]
        qb = (q * (scale * LOG2E)).astype(BF16)
        s = lax.dot_general(kb[:(i + 1) * blk], qb, _NT, preferred_element_type=F32)
        pieces = []
        if i > MOBA_TOPK:
            gate = lax.dot_general(means, q, _NT, preferred_element_type=F32,
                                   precision=lax.Precision.HIGHEST)
            sel = _topk_mask(gate, i, blk_idx)
            for j in range(i):
                bias = jnp.where(sel[j:j + 1, :] > 0.5, 0.0, NEG)
                pieces.append(s[j * blk:(j + 1) * blk] + bias)
        else:
            for j in range(i):
                pieces.append(s[j * blk:(j + 1) * blk])
        pieces.append(jnp.where(causal_t, s[i * blk:], NEG))
        return jnp.concatenate(pieces, axis=0) if len(pieces) > 1 else pieces[0]

    def weighted_values(i, p, l):
        acc = _dot(vt[:, :(i + 1) * blk], p)
        o_ref[i * blk:(i + 1) * blk, :] = (acc / l).T

    sm_next = masked_scores(0)
    pending = None
    for i in range(nb):
        sm = sm_next
        if i + 1 < nb:
            sm_next = masked_scores(i + 1)
        m = jnp.max(sm, axis=0, keepdims=True)
        p = jnp.exp2(sm - m)
        l = jnp.sum(p, axis=0, keepdims=True)
        if pending is not None:
            weighted_values(*pending)
        pending = (i, p.astype(BF16), l)
    weighted_values(*pending)


def _moba_prompt(mq, mk_stack, mv_stack, layer, *, batch, t):
    n = mq.shape[0]
    blk = pl.BlockSpec((t, MOBA_HD), lambda b, h: (b, h))
    kv = pl.BlockSpec((None, t, MOBA_HD), lambda b, h: (layer, b, h))
    return pl.pallas_call(
        _moba_prompt_kernel,
        grid=(batch, MOBA_HEADS),
        in_specs=[blk, kv, kv],
        out_specs=blk,
        out_shape=jax.ShapeDtypeStruct((n, MOBA_D), F32),
        compiler_params=_params("parallel", "parallel"),
        name="moba_prompt",
    )(mq, mk_stack, mv_stack)


_HEAD_TILE = (MOBA_HEADS, MOBA_HD)


def _page_partials(kp, vp, q2):
    s = jnp.sum(kp * q2[None], axis=-1, keepdims=True)
    m = jnp.max(s, axis=0)
    e = jnp.exp2(s - m[None])
    return (jnp.broadcast_to(m, _HEAD_TILE), jnp.broadcast_to(jnp.sum(e, axis=0), _HEAD_TILE),
            jnp.sum(e * vp, axis=0), jnp.sum(kp, axis=0))


def _moba_combine(q, q2, k_new, v_new, partials, pages_per_block, page_len):
    m_pg, l_pg, o_pg, ks_pg = (jnp.stack([p[i] for p in partials]) for i in range(4))
    npg = len(partials)
    nblk = npg // pages_per_block
    ksum = ks_pg.reshape(nblk, pages_per_block, *_HEAD_TILE)
    means = jnp.sum(ksum, axis=1) * (1.0 / (pages_per_block * page_len))
    gate = jnp.broadcast_to(jnp.sum(means * q[None], axis=-1, keepdims=True), (nblk,) + _HEAD_TILE)
    bidx = lax.broadcasted_iota(jnp.int32, (nblk,) + _HEAD_TILE, 0)
    rank = jnp.zeros_like(gate)
    for jp in range(nblk):
        g = gate[jp][None]
        ge = jnp.where(g >= gate, 1.0, 0.0)
        gt = jnp.where(g > gate, 1.0, 0.0)
        rank = rank + jnp.where(bidx > jp, ge, jnp.where(bidx < jp, gt, 0.0))
    sel = jnp.where(rank < MOBA_TOPK, 1.0, 0.0)
    sel = jnp.broadcast_to(sel[:, None], (nblk, pages_per_block) + _HEAD_TILE)
    sel = sel.reshape((npg,) + _HEAD_TILE) > 0.5
    s_self = jnp.broadcast_to(jnp.sum(q2 * k_new, axis=-1, keepdims=True), _HEAD_TILE)
    mp = jnp.where(sel, m_pg, NEG)
    m_all = jnp.maximum(jnp.max(mp, axis=0), s_self)
    w = jnp.where(sel, jnp.exp2(mp - m_all[None]), 0.0)
    w_self = jnp.exp2(s_self - m_all)
    l_all = w_self + jnp.sum(w * l_pg, axis=0)
    o_all = w_self * v_new + jnp.sum(w * o_pg, axis=0)
    return o_all / l_all


def _ffn_moba_kernel(pt_ref, x_ref, g_ref, wg_ref, wu_ref, wd_ref, fn_ref, q_ref, kn_ref, vn_ref,
                     *refs, n_pages, pages_per_block, final_norm):
    del pt_ref
    kp_refs = refs[:n_pages]
    vp_refs = refs[n_pages:2 * n_pages]
    y_ref, o_ref, a_scr = refs[2 * n_pages:]
    x = x_ref[...]
    h = _rms(x, g_ref[...]).astype(BF16)
    q = q_ref[0]
    q2 = q * (MOBA_HD ** -0.5 * LOG2E)
    n_chunks = D_FF // FFN_CHUNK
    page_len = kp_refs[0].shape[0]
    sub = page_len // MOBA_SUBPAGES
    partials = []
    for c in range(max(n_chunks, n_pages)):
        if c < n_chunks:
            sl = slice(c * FFN_CHUNK, (c + 1) * FFN_CHUNK)
            gate = _dot(h, wg_ref[:, sl])
            up = _dot(h, wu_ref[:, sl])
            a_scr[:, sl] = (gate * jax.nn.sigmoid(gate) * up).astype(BF16)
        if c < n_pages:
            for r in range(0, page_len, sub):
                partials.append(_page_partials(kp_refs[c][r:r + sub], vp_refs[c][r:r + sub], q2))
    y = x + FFN_RES * _dot(a_scr[...], wd_ref[...])
    if final_norm:
        y = _rms(y, fn_ref[...])
    y_ref[...] = y
    o_ref[0] = _moba_combine(q, q2, kn_ref[0], vn_ref[0], partials,
                             pages_per_block * (page_len // sub), sub)


def _ffn_moba(x, g, wg, wu, wd, fn, mq, mk, mv, cache_k, cache_v, page_table, layer, seq0, n_seq,
              *, final_norm):
    n = x.shape[0]
    bs, n_pages = page_table.shape
    page = cache_k.shape[2]
    assert n % n_seq == 0 and (n // n_seq) % 8 == 0
    tm = n // n_seq
    r3 = lambda a: a.reshape(bs, MOBA_HEADS, MOBA_HD)
    row = pl.BlockSpec((tm, D_MODEL), lambda i, pt: (i, 0))
    tok = pl.BlockSpec((1, MOBA_HEADS, MOBA_HD), lambda i, pt: (seq0 + i, 0, 0))

    def page_spec(p):
        return pl.BlockSpec((None, None, page, MOBA_HEADS, MOBA_HD),
                            lambda i, pt: (layer, pt[(seq0 + i) * n_pages + p], 0, 0, 0))

    pages = [page_spec(p) for p in range(n_pages)]
    y, o = pl.pallas_call(
        functools.partial(_ffn_moba_kernel, n_pages=n_pages, pages_per_block=MOBA_BLOCK // page,
                          final_norm=final_norm),
        grid_spec=pltpu.PrefetchScalarGridSpec(
            num_scalar_prefetch=1,
            grid=(n_seq,),
            in_specs=[row] + [_layer_resident(a, layer) for a in (g, wg, wu, wd)]
            + [_resident((1, D_MODEL)), tok, tok, tok] + pages + pages,
            out_specs=[row, pl.BlockSpec((1, MOBA_HEADS, MOBA_HD), lambda i, pt: (i, 0, 0))],
            scratch_shapes=[pltpu.VMEM((tm, D_FF), BF16)]),
        out_shape=[jax.ShapeDtypeStruct((n, D_MODEL), F32),
                   jax.ShapeDtypeStruct((n_seq, MOBA_HEADS, MOBA_HD), F32)],
        compiler_params=pltpu.CompilerParams(dimension_semantics=("parallel",),
                                             vmem_limit_bytes=VMEM_LIMIT_FUSED),
        name="ffn_moba",
    )(page_table.reshape(-1), x, g, wg, wu, wd, fn, r3(mq), r3(mk), r3(mv),
      *([cache_k] * n_pages), *([cache_v] * n_pages))
    return y, o.reshape(n_seq, MOBA_D)


def _outproj_kernel(x_ref, og_ref, om_ref, sa_ref, sb_ref, wg_ref, wm_ref, wo_ref, o_ref):
    ya = _dot(og_ref[...].astype(BF16), wg_ref[...])
    yb = _dot(om_ref[...].astype(BF16), wm_ref[...])
    merged = sa_ref[...] * ya + sb_ref[...] * yb
    o_ref[...] = x_ref[...] + _dot(merged.astype(BF16), wo_ref[...])


def _outproj(x, og, om, sa, sb, wg, wm, wo, layer, *, tm):
    n = x.shape[0]
    row = pl.BlockSpec((tm, D_MODEL), lambda i: (i, 0))
    return pl.pallas_call(
        _outproj_kernel,
        grid=(n // tm,),
        in_specs=[row] * 5 + [_layer_resident(a, layer) for a in (wg, wm, wo)],
        out_specs=row,
        out_shape=jax.ShapeDtypeStruct((n, D_MODEL), F32),
        compiler_params=_params("parallel"),
        name="outproj",
    )(x, og, om, sa, sb, wg, wm, wo)


def _rope_tables(pos):
    half = ROT_DIM // 2
    inv = ROPE_THETA ** (-jnp.arange(half, dtype=F32) * 2.0 / ROT_DIM)
    ang = pos.astype(F32)[:, None] * inv[None, :]
    cos, sin = jnp.cos(ang), jnp.sin(ang)
    n = pos.shape[0]
    zeros = lambda w: jnp.zeros((n, w), F32)
    ra = jnp.concatenate([cos, cos, jnp.ones((n, MOBA_HD - ROT_DIM), F32)], axis=1)
    rb1 = jnp.concatenate([-sin, zeros(MOBA_HD - half)], axis=1)
    rb2 = jnp.concatenate([zeros(half), sin, zeros(MOBA_HD - ROT_DIM)], axis=1)
    return ra, rb1, rb2


def _chunk_tri(tm):
    r = jnp.arange(tm)
    same = (r[:, None] // GLA_CHUNK) == (r[None, :] // GLA_CHUNK)
    return (same & (r[:, None] >= r[None, :])).astype(BF16)


def _split_w_in(w_in, w_alpha, b_alpha):
    sizes = (GLA_DK, GLA_DK, GLA_DV, GLA_DV, GLA_GATE_RANK, MOBA_D, MOBA_D, MOBA_D, D_MODEL, D_MODEL)
    offs = [0]
    for s in sizes:
        offs.append(offs[-1] + s)
    parts = [w_in[:, offs[i]:offs[i + 1]].astype(BF16) for i in range(len(sizes))]
    wgq, wgk, wgv, wgr, wga, wmq, wmk, wmv, wa, wb = parts
    pad = LANES - GLA_GATE_RANK
    wga = jnp.pad(wga, ((0, 0), (0, pad)))
    wal = jnp.pad(w_alpha.astype(BF16), ((0, pad), (0, 0)))
    return (wgq, wgk, wgv, wgr, wga, wal, b_alpha.reshape(1, GLA_DK), wmq, wmk, wmv, wa, wb)


def kernel(x_prompt, x_sample, cache_k, cache_v, state_gla, page_table, ffn1_norm, ffn1_w_gate, ffn1_w_up, ffn1_w_down, mix_norm, w_in, gla_w_alpha, gla_b_alpha, gla_out_norm, w_o_gla, w_o_moba, w_out, ffn2_norm, ffn2_w_gate, ffn2_w_up, ffn2_w_down, final_norm):
    batch, t, _ = x_prompt.shape
    bs = x_sample.shape[0]
    depth = w_in.shape[0]
    past_len = page_table.shape[1] * cache_k.shape[2]
    tm_p, tm_in = 512, 256
    tm_s = bs

    xp = x_prompt.reshape(batch * t, D_MODEL)
    xs = x_sample.reshape(bs, D_MODEL)
    rope_p = _rope_tables(jnp.arange(t))
    rope_s = _rope_tables(jnp.full((tm_s,), past_len))
    tri_p = _chunk_tri(tm_in)
    tri_s = jnp.zeros((8, LANES), BF16)
    fn = final_norm.reshape(1, D_MODEL)
    row = lambda a: a.reshape(1, -1)

    bf16 = lambda a: a.astype(BF16)
    f1 = (ffn1_norm[:, None, :], bf16(ffn1_w_gate), bf16(ffn1_w_up), bf16(ffn1_w_down))
    f2 = (ffn2_norm[:, None, :], bf16(ffn2_w_gate), bf16(ffn2_w_up), bf16(ffn2_w_down))
    w_o = (bf16(w_o_gla), bf16(w_o_moba), bf16(w_out))
    mixn = mix_norm[:, None, :]

    sp_l, ks_l, vs_l = [], [], []
    kv_stack, s_stack = None, None
    for l in range(depth):
        last = l == depth - 1
        gn = row(gla_out_norm[l])
        w_proj = _split_w_in(w_in[l], gla_w_alpha[l], gla_b_alpha[l])

        xs = _ffn(xs, *f1, fn, l, tm=tm_s, final_norm=False)
        gq_s, gk_s, gv_s, sgr_s, la_s, mq_s, mk_s, mv_s, sa_s, sb_s = _inproj(
            xs, mixn, w_proj, rope_s, tri_s, tm=tm_s, n_pos_tiles=1, cum=False, layer=l)
        moba_s = (mq_s, mk_s, mv_s, cache_k, cache_v, page_table, l)
        half = bs // 2

        xp, om_s0 = _ffn_moba(xp, *f1, fn, *moba_s, 0, half, final_norm=False)
        gq, gk, gv, sgr, bl, mq, mk, mv, sa, sb = _inproj(
            xp, mixn, w_proj, rope_p, tri_p, tm=tm_in, n_pos_tiles=t // tm_in, cum=True,
            kv_stack=kv_stack, layer=l, depth=depth)
        kv_stack = (mk, mv)
        og, s_p = _gla_prompt(gq, gk, bl, gv, sgr, gn, batch=batch, t=t)
        om = _moba_prompt(mq, mk, mv, l, batch=batch, t=t)
        xp = _outproj(xp, og, om, sa, sb, *w_o, l, tm=tm_p)
        xp, om_s1 = _ffn_moba(xp, *f2, fn, *moba_s, half, bs - half, final_norm=last)
        sp_l.append(s_p)

        og_s, s_stack = _gla_sample(gq_s, gk_s, la_s, gv_s, sgr_s, gn, state_gla, s_stack, l)
        om_s = jnp.concatenate([om_s0, om_s1], axis=0)
        xs = _outproj(xs, og_s, om_s, sa_s, sb_s, *w_o, l, tm=tm_s)
        xs = _ffn(xs, *f2, fn, l, tm=tm_s, final_norm=last)
        ks_l.append(mk_s.reshape(bs, 1, MOBA_HEADS, MOBA_HD))
        vs_l.append(mv_s.reshape(bs, 1, MOBA_HEADS, MOBA_HD))

    kv_shape = (depth, batch, t, MOBA_HEADS, MOBA_HD)
    return (xp.reshape(batch, t, D_MODEL), xs.reshape(bs, 1, D_MODEL),
            kv_stack[0].reshape(kv_shape), kv_stack[1].reshape(kv_shape), jnp.stack(sp_l),
            jnp.stack(ks_l), jnp.stack(vs_l), s_stack)
```

```python
import functools

import jax
import jax.numpy as jnp
from jax import lax
from jax.experimental import pallas as pl
from jax.experimental.pallas import tpu as pltpu

F32 = jnp.float32
BF16 = jnp.bfloat16

D_MODEL = 1024
GLA_HEADS = 4
GLA_DK = D_MODEL // 2
GLA_DV = D_MODEL
GLA_HDK = GLA_DK // GLA_HEADS
GLA_HDV = GLA_DV // GLA_HEADS
GLA_GATE_RANK = 16
GLA_GATE_TEMP = 16.0
GLA_LOGA_MIN = -4.0
MOBA_HEADS = 8
MOBA_HD = D_MODEL // MOBA_HEADS
MOBA_D = MOBA_HEADS * MOBA_HD
MOBA_BLOCK = 256
MOBA_TOPK = 3
ROT_DIM = MOBA_HD // 4
ROPE_THETA = 500000.0
D_FF = 2816
FFN_RES = 0.5
NORM_EPS = 1e-6

LANES = 128
FFN_CHUNK = 256
GLA_CHUNK = 32
GLA_MID = GLA_CHUNK // 2 - 1
GLA_UNROLL = 16
GLA_SAMPLE_ROWS = 8
MOBA_PROMPT_HEADS = 2
MOBA_SUBPAGES = 4
LOG2E = 1.4426950408889634
NEG = -1e30
VMEM_LIMIT = 56 * 1024 * 1024
VMEM_LIMIT_FUSED = 62 * 1024 * 1024

_NT = (((1,), (1,)), ((), ()))
_TN = (((0,), (0,)), ((), ()))


def _rms(x, g):
    return x * lax.rsqrt(jnp.mean(x * x, axis=-1, keepdims=True) + NORM_EPS) * g


def _dot(a, b):
    return jnp.dot(a, b, preferred_element_type=F32)


def _resident(shape):
    nd = len(shape)
    return pl.BlockSpec(shape, lambda *_: (0,) * nd, pipeline_mode=pl.Buffered(1))


def _layer_resident(stack, layer):
    return pl.BlockSpec((None,) + stack.shape[1:], lambda *_: (layer, 0, 0),
                        pipeline_mode=pl.Buffered(1))


def _params(*sem):
    return pltpu.CompilerParams(dimension_semantics=sem, vmem_limit_bytes=VMEM_LIMIT)


def _ffn_kernel(x_ref, g_ref, wg_ref, wu_ref, wd_ref, fn_ref, o_ref, a_scr, *, final_norm):
    x = x_ref[...]
    h = _rms(x, g_ref[...]).astype(BF16)
    for c in range(D_FF // FFN_CHUNK):
        sl = slice(c * FFN_CHUNK, (c + 1) * FFN_CHUNK)
        g = _dot(h, wg_ref[:, sl])
        u = _dot(h, wu_ref[:, sl])
        a_scr[:, sl] = (g * jax.nn.sigmoid(g) * u).astype(BF16)
    y = x + FFN_RES * _dot(a_scr[...], wd_ref[...])
    if final_norm:
        y = _rms(y, fn_ref[...])
    o_ref[...] = y


def _ffn(x, g, wg, wu, wd, fn, layer, *, tm, final_norm):
    n = x.shape[0]
    row = pl.BlockSpec((tm, D_MODEL), lambda i: (i, 0))
    return pl.pallas_call(
        functools.partial(_ffn_kernel, final_norm=final_norm),
        grid=(n // tm,),
        in_specs=[row] + [_layer_resident(a, layer) for a in (g, wg, wu, wd)]
        + [_resident((1, D_MODEL))],
        out_specs=row,
        out_shape=jax.ShapeDtypeStruct((n, D_MODEL), F32),
        scratch_shapes=[pltpu.VMEM((tm, D_FF), BF16)],
        compiler_params=_params("parallel"),
        name="ffn",
    )(x, g, wg, wu, wd, fn)


def _rope(y, ra, rb1, rb2):
    half = ROT_DIM // 2
    return (y * ra + pltpu.roll(y, LANES - half, 1) * rb1 + pltpu.roll(y, half, 1) * rb2)


def _inproj_kernel(x_ref, g_ref, wgq_ref, wgk_ref, wgv_ref, wgr_ref, wga_ref, wal_ref, bal_ref,
                   wmq_ref, wmk_ref, wmv_ref, wa_ref, wb_ref, ra_ref, rb1_ref, rb2_ref, tri_ref,
                   *rest, cum):
    gq_ref, gk_ref, gv_ref, sgr_ref, bl_ref, mq_ref, mk_ref, mv_ref, sa_ref, sb_ref = rest[-10:]
    h = _rms(x_ref[...], g_ref[...]).astype(BF16)
    ga = _dot(h, wga_ref[...]).astype(BF16)
    z = _dot(ga, wal_ref[...]) + bal_ref[...]
    log_a = (jnp.minimum(z, 0.0) - jnp.log(1.0 + jnp.exp(-jnp.abs(z)))) * (1.0 / GLA_GATE_TEMP)
    log_a = jnp.maximum(log_a, GLA_LOGA_MIN)
    sa_ref[...] = jax.nn.sigmoid(_dot(h, wa_ref[...]))
    sb_ref[...] = jax.nn.sigmoid(_dot(h, wb_ref[...]))
    gr = _dot(h, wgr_ref[...])
    sgr_ref[...] = gr * jax.nn.sigmoid(gr)
    if cum:
        hi = log_a.astype(BF16)
        lo = (log_a - hi.astype(F32)).astype(BF16)
        tri = tri_ref[...]
        bl_ref[...] = _dot(tri, hi) + _dot(tri, lo)
    else:
        bl_ref[...] = log_a
    ra, rb1, rb2 = ra_ref[...], rb1_ref[...], rb2_ref[...]
    mq = _dot(h, wmq_ref[...])
    mk = _dot(h, wmk_ref[...])
    for hd in range(MOBA_HEADS):
        sl = slice(hd * MOBA_HD, (hd + 1) * MOBA_HD)
        mq_ref[:, sl] = _rope(mq[:, sl], ra, rb1, rb2)
        mk_ref[:, sl] = _rope(mk[:, sl], ra, rb1, rb2)
    gq_ref[...] = _dot(h, wgq_ref[...])
    gk_ref[...] = _dot(h, wgk_ref[...])
    gv_ref[...] = _dot(h, wgv_ref[...])
    mv_ref[...] = _dot(h, wmv_ref[...])


def _inproj(x, g, w, rope, tri, *, tm, n_pos_tiles, cum, kv_stack=None, layer=0, depth=None):
    n = x.shape[0]
    row = lambda width: pl.BlockSpec((tm, width), lambda i: (i, 0))
    pos = pl.BlockSpec((tm, LANES), lambda i: (i % n_pos_tiles, 0))
    widths = (GLA_DK, GLA_DK, GLA_DV, GLA_DV, GLA_DK, MOBA_D, MOBA_D, MOBA_D, D_MODEL, D_MODEL)
    out_specs = [row(wd) for wd in widths]
    out_shape = [jax.ShapeDtypeStruct((n, wd), F32) for wd in widths]
    in_specs = ([row(D_MODEL), _layer_resident(g, layer)] + [_resident(a.shape) for a in w]
                + [pos, pos, pos, _resident(tri.shape)])
    args = [x, g, *w, *rope, tri]
    aliases = {}
    if depth is not None:
        for o_idx in (6, 7):
            out_specs[o_idx] = pl.BlockSpec((None, tm, MOBA_D), lambda i: (layer, i, 0))
            out_shape[o_idx] = jax.ShapeDtypeStruct((depth, n, MOBA_D), F32)
        if kv_stack is not None:
            for o_idx, a in zip((6, 7), kv_stack):
                in_specs.append(pl.BlockSpec(memory_space=pl.ANY))
                args.append(a)
                aliases[len(args) - 1] = o_idx
    return pl.pallas_call(
        functools.partial(_inproj_kernel, cum=cum),
        grid=(n // tm,),
        in_specs=in_specs,
        out_specs=out_specs,
        out_shape=out_shape,
        input_output_aliases=aliases,
        compiler_params=_params("parallel"),
        name="inproj",
    )(*args)


def _gla_prompt_kernel(q_ref, k_ref, b_ref, v_ref, sgr_ref, gn_ref, o_ref, s_ref, st_scr):
    t = q_ref.shape[0]
    c_len = GLA_CHUNK
    st_scr[...] = jnp.zeros_like(st_scr)
    causal = (lax.broadcasted_iota(jnp.int32, (c_len, c_len), 0)
              >= lax.broadcasted_iota(jnp.int32, (c_len, c_len), 1))
    scale = GLA_HDK ** -0.5
    gn = gn_ref[...]

    def decayed(rows):
        q, k, b = q_ref[rows, :], k_ref[rows, :], b_ref[rows, :]
        b_mid = b[GLA_MID:GLA_MID + 1, :]
        b_last = b[c_len - 1:c_len, :]
        qt = q * jnp.exp(b - b_mid) * scale
        kt = k * jnp.exp(b_mid - b)
        att = lax.dot_general(qt.astype(BF16), kt.astype(BF16), _NT, preferred_element_type=F32)
        att = jnp.where(causal, att, 0.0).astype(BF16)
        qs = (qt * jnp.exp(b_mid)).astype(BF16)
        ke = (kt * jnp.exp(b_last - b_mid)).astype(BF16)
        return att, qs, ke, jnp.exp(b_last)

    def local(rows, att, ke):
        vb = v_ref[rows, :].astype(BF16)
        return _dot(att, vb), lax.dot_general(vb, ke, _TN, preferred_element_type=F32)

    def body(i, carry):
        rows = [pl.ds(pl.multiple_of((i * GLA_UNROLL + u) * c_len, c_len), c_len)
                for u in range(GLA_UNROLL)]
        pre = [decayed(r) for r in rows]
        loc = [local(r, att, ke) for r, (att, _, ke, _) in zip(rows, pre)]
        st = st_scr[...]
        for r, (_, qs, _, d_last), (o_intra, kv) in zip(rows, pre, loc):
            o = o_intra + lax.dot_general(qs, st.astype(BF16), _NT, preferred_element_type=F32)
            o_ref[r, :] = _rms(o, gn) * sgr_ref[r, :]
            st = st * d_last + kv
        st_scr[...] = st
        return carry

    lax.fori_loop(0, t // (c_len * GLA_UNROLL), body, 0)
    s_ref[0, 0] = st_scr[...].T


def _gla_prompt(gq, gk, bl, gv, sgr, gn, *, batch, t):
    n = gq.shape[0]
    kblk = pl.BlockSpec((t, GLA_HDK), lambda b, h: (b, h))
    vblk = pl.BlockSpec((t, GLA_HDV), lambda b, h: (b, h))
    return pl.pallas_call(
        _gla_prompt_kernel,
        grid=(batch, GLA_HEADS),
        in_specs=[kblk, kblk, kblk, vblk, vblk, pl.BlockSpec((1, GLA_HDV), lambda b, h: (0, 0))],
        out_specs=[vblk, pl.BlockSpec((1, 1, GLA_HDK, GLA_HDV), lambda b, h: (b, h, 0, 0))],
        out_shape=[jax.ShapeDtypeStruct((n, GLA_DV), F32),
                   jax.ShapeDtypeStruct((batch, GLA_HEADS, GLA_HDK, GLA_HDV), F32)],
        scratch_shapes=[pltpu.VMEM((GLA_HDV, GLA_HDK), F32)],
        compiler_params=_params("parallel", "parallel"),
        name="gla_prompt",
    )(gq, gk, bl, gv, sgr, gn)


def _column(r, width):
    col = jnp.broadcast_to(r, (LANES, LANES)).T
    return jnp.concatenate([col] * (width // LANES), axis=1)


def _gla_sample_kernel(q_ref, k_ref, la_ref, v_ref, sgr_ref, gn_ref, s_ref, *rest):
    o_ref, sn_ref = rest[-2:]
    scale = GLA_HDK ** -0.5
    gn = gn_ref[...]
    for r in range(q_ref.shape[1]):
        row = slice(r, r + 1)
        for h in range(GLA_HEADS):
            ks = slice(h * GLA_HDK, (h + 1) * GLA_HDK)
            vs = slice(h * GLA_HDV, (h + 1) * GLA_HDV)
            q, k, la = q_ref[0, row, ks], k_ref[0, row, ks], la_ref[0, row, ks]
            v = v_ref[0, row, vs]
            s_new = _column(jnp.exp(la), GLA_HDV) * s_ref[r, h] + _column(k, GLA_HDV) * v
            sn_ref[r, h] = s_new
            q8 = jnp.broadcast_to(q * scale, (8, GLA_HDK)).astype(BF16)
            o = _dot(q8, s_new.astype(BF16))[0:1, :]
            o_ref[0, row, vs] = _rms(o, gn) * sgr_ref[0, row, vs]


def _gla_sample(gq, gk, la, gv, sgr, gn, state_all, state_out, layer):
    bs = gq.shape[0]
    rows = GLA_SAMPLE_ROWS
    r3 = lambda a: a.reshape(bs // rows, rows, a.shape[-1])
    kblk = pl.BlockSpec((1, rows, GLA_DK), lambda b: (b, 0, 0))
    vblk = pl.BlockSpec((1, rows, GLA_DV), lambda b: (b, 0, 0))
    sblk = pl.BlockSpec((None, rows, GLA_HEADS, GLA_HDK, GLA_HDV), lambda b: (layer, b, 0, 0, 0))
    in_specs = [kblk, kblk, kblk, vblk, vblk, pl.BlockSpec((1, GLA_HDV), lambda b: (0, 0)), sblk]
    args = [r3(gq), r3(gk), r3(la), r3(gv), r3(sgr), gn, state_all]
    aliases = {}
    if state_out is not None:
        in_specs.append(pl.BlockSpec(memory_space=pl.ANY))
        args.append(state_out)
        aliases = {len(args) - 1: 1}
    o, s_new = pl.pallas_call(
        _gla_sample_kernel,
        grid=(bs // rows,),
        in_specs=in_specs,
        out_specs=[vblk, sblk],
        out_shape=[jax.ShapeDtypeStruct((bs // rows, rows, GLA_DV), F32),
                   jax.ShapeDtypeStruct(state_all.shape, F32)],
        input_output_aliases=aliases,
        compiler_params=_params("parallel"),
        name="gla_sample",
    )(*args)
    return o.reshape(bs, GLA_DV), s_new


def _topk_mask(gate, n_valid, blk_idx):
    rank = jnp.zeros_like(gate)
    for jp in range(n_valid):
        row = gate[jp:jp + 1, :]
        ge = jnp.where(row >= gate, 1.0, 0.0)
        gt = jnp.where(row > gate, 1.0, 0.0)
        rank = rank + jnp.where(blk_idx > jp, ge, jnp.where(blk_idx < jp, gt, 0.0))
    return jnp.where(rank < MOBA_TOPK, 1.0, 0.0)


def _moba_prompt_kernel(q_ref, k_ref, v_ref, o_ref):
    t = q_ref.shape[0]
    blk = MOBA_BLOCK
    nb = t // blk
    n_heads = q_ref.shape[1] // MOBA_HD
    scale = MOBA_HD ** -0.5
    lanes = [slice(hh * MOBA_HD, (hh + 1) * MOBA_HD) for hh in range(n_heads)]
    ks = [k_ref[:, sl] for sl in lanes]
    kbs = [k.astype(BF16) for k in ks]
    vts = [v_ref[:, sl].T.astype(BF16) for sl in lanes]
    means_h = [jnp.sum(k.reshape(nb, blk, MOBA_HD), axis=1) * (1.0 / blk) for k in ks]
    causal_t = (lax.broadcasted_iota(jnp.int32, (blk, blk), 0)
                <= lax.broadcasted_iota(jnp.int32, (blk, blk), 1))
    blk_idx = lax.broadcasted_iota(jnp.int32, (nb, blk), 0)

    def masked_scores(i, hh):
        q = q_ref[i * blk:(i + 1) * blk, lanes[hh]]
        qb = (q * (scale * LOG2E)).astype(BF16)
        s = lax.dot_general(kbs[hh][:(i + 1) * blk], qb, _NT,
                            preferred_element_type=F32)
        pieces = []
        if i > MOBA_TOPK:
            gate = lax.dot_general(means_h[hh], q, _NT, preferred_element_type=F32,
                                   precision=lax.Precision.HIGHEST)
            sel = _topk_mask(gate, i, blk_idx)
            for j in range(i):
                bias = jnp.where(sel[j:j + 1, :] > 0.5, 0.0, NEG)
                pieces.append(s[j * blk:(j + 1) * blk] + bias)
        else:
            for j in range(i):
                pieces.append(s[j * blk:(j + 1) * blk])
        pieces.append(jnp.where(causal_t, s[i * blk:], NEG))
        return jnp.concatenate(pieces, axis=0) if len(pieces) > 1 else pieces[0]

    def weighted_values(i, hh, p, l):
        acc = _dot(vts[hh][:, :(i + 1) * blk], p)
        o_ref[i * blk:(i + 1) * blk, lanes[hh]] = (acc / l).T

    items = [(i, hh) for i in range(nb) for hh in range(n_heads)]
    sm_next = masked_scores(*items[0])
    pending = None
    for n, item in enumerate(items):
        sm = sm_next
        if n + 1 < len(items):
            sm_next = masked_scores(*items[n + 1])
        m = jnp.max(sm, axis=0, keepdims=True)
        p = jnp.exp2(sm - m)
        l = jnp.sum(p, axis=0, keepdims=True)
        if pending is not None:
            weighted_values(*pending)
        pending = (*item, p.astype(BF16), l)
    weighted_values(*pending)


def _moba_prompt(mq, mk_stack, mv_stack, layer, *, batch, t):
    n = mq.shape[0]
    width = MOBA_PROMPT_HEADS * MOBA_HD
    blk = pl.BlockSpec((t, width), lambda b, h: (b, h))
    kv = pl.BlockSpec((None, t, width), lambda b, h: (layer, b, h))
    return pl.pallas_call(
        _moba_prompt_kernel,
        grid=(batch, MOBA_HEADS // MOBA_PROMPT_HEADS),
        in_specs=[blk, kv, kv],
        out_specs=blk,
        out_shape=jax.ShapeDtypeStruct((n, MOBA_D), F32),
        compiler_params=_params("parallel", "parallel"),
        name="moba_prompt",
    )(mq, mk_stack, mv_stack)


_HEAD_TILE = (MOBA_HEADS, MOBA_HD)


def _page_partials(kp, vp, q2):
    s = jnp.sum(kp * q2[None], axis=-1, keepdims=True)
    m = jnp.max(s, axis=0)
    e = jnp.exp2(s - m[None])
    return (jnp.broadcast_to(m, _HEAD_TILE), jnp.broadcast_to(jnp.sum(e, axis=0), _HEAD_TILE),
            jnp.sum(e * vp, axis=0), jnp.sum(kp, axis=0))


def _moba_combine(q, q2, k_new, v_new, partials, pages_per_block, page_len):
    m_pg, l_pg, o_pg, ks_pg = (jnp.stack([p[i] for p in partials]) for i in range(4))
    npg = len(partials)
    nblk = npg // pages_per_block
    ksum = ks_pg.reshape(nblk, pages_per_block, *_HEAD_TILE)
    means = jnp.sum(ksum, axis=1) * (1.0 / (pages_per_block * page_len))
    gate = jnp.broadcast_to(jnp.sum(means * q[None], axis=-1, keepdims=True), (nblk,) + _HEAD_TILE)
    bidx = lax.broadcasted_iota(jnp.int32, (nblk,) + _HEAD_TILE, 0)
    rank = jnp.zeros_like(gate)
    for jp in range(nblk):
        g = gate[jp][None]
        ge = jnp.where(g >= gate, 1.0, 0.0)
        gt = jnp.where(g > gate, 1.0, 0.0)
        rank = rank + jnp.where(bidx > jp, ge, jnp.where(bidx < jp, gt, 0.0))
    sel = jnp.where(rank < MOBA_TOPK, 1.0, 0.0)
    sel = jnp.broadcast_to(sel[:, None], (nblk, pages_per_block) + _HEAD_TILE)
    sel = sel.reshape((npg,) + _HEAD_TILE) > 0.5
    s_self = jnp.broadcast_to(jnp.sum(q2 * k_new, axis=-1, keepdims=True), _HEAD_TILE)
    mp = jnp.where(sel, m_pg, NEG)
    m_all = jnp.maximum(jnp.max(mp, axis=0), s_self)
    w = jnp.where(sel, jnp.exp2(mp - m_all[None]), 0.0)
    w_self = jnp.exp2(s_self - m_all)
    l_all = w_self + jnp.sum(w * l_pg, axis=0)
    o_all = w_self * v_new + jnp.sum(w * o_pg, axis=0)
    return o_all / l_all


def _ffn_moba_kernel(pt_ref, x_ref, g_ref, wg_ref, wu_ref, wd_ref, fn_ref, q_ref, kn_ref, vn_ref,
                     *refs, n_pages, pages_per_block, final_norm):
    del pt_ref
    kp_refs = refs[:n_pages]
    vp_refs = refs[n_pages:2 * n_pages]
    y_ref, o_ref, a_scr = refs[2 * n_pages:]
    x = x_ref[...]
    h = _rms(x, g_ref[...]).astype(BF16)
    q = q_ref[0]
    q2 = q * (MOBA_HD ** -0.5 * LOG2E)
    n_chunks = D_FF // FFN_CHUNK
    page_len = kp_refs[0].shape[0]
    sub = page_len // MOBA_SUBPAGES
    partials = []
    for c in range(max(n_chunks, n_pages)):
        if c < n_chunks:
            sl = slice(c * FFN_CHUNK, (c + 1) * FFN_CHUNK)
            gate = _dot(h, wg_ref[:, sl])
            up = _dot(h, wu_ref[:, sl])
            a_scr[:, sl] = (gate * jax.nn.sigmoid(gate) * up).astype(BF16)
        if c < n_pages:
            for r in range(0, page_len, sub):
                partials.append(_page_partials(kp_refs[c][r:r + sub], vp_refs[c][r:r + sub], q2))
    y = x + FFN_RES * _dot(a_scr[...], wd_ref[...])
    if final_norm:
        y = _rms(y, fn_ref[...])
    y_ref[...] = y
    o_ref[0] = _moba_combine(q, q2, kn_ref[0], vn_ref[0], partials,
                             pages_per_block * (page_len // sub), sub)


def _ffn_moba(x, g, wg, wu, wd, fn, mq, mk, mv, cache_k, cache_v, page_table, layer, seq0, n_seq,
              *, final_norm):
    n = x.shape[0]
    bs, n_pages = page_table.shape
    page = cache_k.shape[2]
    assert n % n_seq == 0 and (n // n_seq) % 8 == 0
    tm = n // n_seq
    r3 = lambda a: a.reshape(bs, MOBA_HEADS, MOBA_HD)
    row = pl.BlockSpec((tm, D_MODEL), lambda i, pt: (i, 0))
    tok = pl.BlockSpec((1, MOBA_HEADS, MOBA_HD), lambda i, pt: (seq0 + i, 0, 0))

    def page_spec(p):
        return pl.BlockSpec((None, None, page, MOBA_HEADS, MOBA_HD),
                            lambda i, pt: (layer, pt[(seq0 + i) * n_pages + p], 0, 0, 0))

    pages = [page_spec(p) for p in range(n_pages)]
    y, o = pl.pallas_call(
        functools.partial(_ffn_moba_kernel, n_pages=n_pages, pages_per_block=MOBA_BLOCK // page,
                          final_norm=final_norm),
        grid_spec=pltpu.PrefetchScalarGridSpec(
            num_scalar_prefetch=1,
            grid=(n_seq,),
            in_specs=[row] + [_layer_resident(a, layer) for a in (g, wg, wu, wd)]
            + [_resident((1, D_MODEL)), tok, tok, tok] + pages + pages,
            out_specs=[row, pl.BlockSpec((1, MOBA_HEADS, MOBA_HD), lambda i, pt: (i, 0, 0))],
            scratch_shapes=[pltpu.VMEM((tm, D_FF), BF16)]),
        out_shape=[jax.ShapeDtypeStruct((n, D_MODEL), F32),
                   jax.ShapeDtypeStruct((n_seq, MOBA_HEADS, MOBA_HD), F32)],
        compiler_params=pltpu.CompilerParams(dimension_semantics=("parallel",),
                                             vmem_limit_bytes=VMEM_LIMIT_FUSED),
        name="ffn_moba",
    )(page_table.reshape(-1), x, g, wg, wu, wd, fn, r3(mq), r3(mk), r3(mv),
      *([cache_k] * n_pages), *([cache_v] * n_pages))
    return y, o.reshape(n_seq, MOBA_D)


def _outproj_kernel(x_ref, og_ref, om_ref, sa_ref, sb_ref, wg_ref, wm_ref, wo_ref, o_ref):
    ya = _dot(og_ref[...].astype(BF16), wg_ref[...])
    yb = _dot(om_ref[...].astype(BF16), wm_ref[...])
    merged = sa_ref[...] * ya + sb_ref[...] * yb
    o_ref[...] = x_ref[...] + _dot(merged.astype(BF16), wo_ref[...])


def _outproj(x, og, om, sa, sb, wg, wm, wo, layer, *, tm):
    n = x.shape[0]
    row = pl.BlockSpec((tm, D_MODEL), lambda i: (i, 0))
    return pl.pallas_call(
        _outproj_kernel,
        grid=(n // tm,),
        in_specs=[row] * 5 + [_layer_resident(a, layer) for a in (wg, wm, wo)],
        out_specs=row,
        out_shape=jax.ShapeDtypeStruct((n, D_MODEL), F32),
        compiler_params=_params("parallel"),
        name="outproj",
    )(x, og, om, sa, sb, wg, wm, wo)


def _rope_tables(pos):
    half = ROT_DIM // 2
    inv = ROPE_THETA ** (-jnp.arange(half, dtype=F32) * 2.0 / ROT_DIM)
    ang = pos.astype(F32)[:, None] * inv[None, :]
    cos, sin = jnp.cos(ang), jnp.sin(ang)
    n = pos.shape[0]
    zeros = lambda w: jnp.zeros((n, w), F32)
    ra = jnp.concatenate([cos, cos, jnp.ones((n, MOBA_HD - ROT_DIM), F32)], axis=1)
    rb1 = jnp.concatenate([-sin, zeros(MOBA_HD - half)], axis=1)
    rb2 = jnp.concatenate([zeros(half), sin, zeros(MOBA_HD - ROT_DIM)], axis=1)
    return ra, rb1, rb2


def _chunk_tri(tm):
    r = jnp.arange(tm)
    same = (r[:, None] // GLA_CHUNK) == (r[None, :] // GLA_CHUNK)
    return (same & (r[:, None] >= r[None, :])).astype(BF16)


def _split_w_in(w_in, w_alpha, b_alpha):
    sizes = (GLA_DK, GLA_DK, GLA_DV, GLA_DV, GLA_GATE_RANK, MOBA_D, MOBA_D, MOBA_D, D_MODEL, D_MODEL)
    offs = [0]
    for s in sizes:
        offs.append(offs[-1] + s)
    parts = [w_in[:, offs[i]:offs[i + 1]].astype(BF16) for i in range(len(sizes))]
    wgq, wgk, wgv, wgr, wga, wmq, wmk, wmv, wa, wb = parts
    pad = LANES - GLA_GATE_RANK
    wga = jnp.pad(wga, ((0, 0), (0, pad)))
    wal = jnp.pad(w_alpha.astype(BF16), ((0, pad), (0, 0)))
    return (wgq, wgk, wgv, wgr, wga, wal, b_alpha.reshape(1, GLA_DK), wmq, wmk, wmv, wa, wb)


def kernel(x_prompt, x_sample, cache_k, cache_v, state_gla, page_table, ffn1_norm, ffn1_w_gate, ffn1_w_up, ffn1_w_down, mix_norm, w_in, gla_w_alpha, gla_b_alpha, gla_out_norm, w_o_gla, w_o_moba, w_out, ffn2_norm, ffn2_w_gate, ffn2_w_up, ffn2_w_down, final_norm):
    batch, t, _ = x_prompt.shape
    bs = x_sample.shape[0]
    depth = w_in.shape[0]
    past_len = page_table.shape[1] * cache_k.shape[2]
    tm_p, tm_in = 512, 256
    tm_s = bs

    xp = x_prompt.reshape(batch * t, D_MODEL)
    xs = x_sample.reshape(bs, D_MODEL)
    rope_p = _rope_tables(jnp.arange(t))
    rope_s = _rope_tables(jnp.full((tm_s,), past_len))
    tri_p = _chunk_tri(tm_in)
    tri_s = jnp.zeros((8, LANES), BF16)
    fn = final_norm.reshape(1, D_MODEL)
    row = lambda a: a.reshape(1, -1)

    bf16 = lambda a: a.astype(BF16)
    f1 = (ffn1_norm[:, None, :], bf16(ffn1_w_gate), bf16(ffn1_w_up), bf16(ffn1_w_down))
    f2 = (ffn2_norm[:, None, :], bf16(ffn2_w_gate), bf16(ffn2_w_up), bf16(ffn2_w_down))
    w_o = (bf16(w_o_gla), bf16(w_o_moba), bf16(w_out))
    mixn = mix_norm[:, None, :]

    sp_l, ks_l, vs_l = [], [], []
    kv_stack, s_stack = None, None
    for l in range(depth):
        last = l == depth - 1
        gn = row(gla_out_norm[l])
        w_proj = _split_w_in(w_in[l], gla_w_alpha[l], gla_b_alpha[l])

        xs = _ffn(xs, *f1, fn, l, tm=tm_s, final_norm=False)
        gq_s, gk_s, gv_s, sgr_s, la_s, mq_s, mk_s, mv_s, sa_s, sb_s = _inproj(
            xs, mixn, w_proj, rope_s, tri_s, tm=tm_s, n_pos_tiles=1, cum=False, layer=l)
        moba_s = (mq_s, mk_s, mv_s, cache_k, cache_v, page_table, l)
        half = bs // 2

        xp, om_s0 = _ffn_moba(xp, *f1, fn, *moba_s, 0, half, final_norm=False)
        gq, gk, gv, sgr, bl, mq, mk, mv, sa, sb = _inproj(
            xp, mixn, w_proj, rope_p, tri_p, tm=tm_in, n_pos_tiles=t // tm_in, cum=True,
            kv_stack=kv_stack, layer=l, depth=depth)
        kv_stack = (mk, mv)
        og, s_p = _gla_prompt(gq, gk, bl, gv, sgr, gn, batch=batch, t=t)
        om = _moba_prompt(mq, mk, mv, l, batch=batch, t=t)
        xp = _outproj(xp, og, om, sa, sb, *w_o, l, tm=tm_p)
        xp, om_s1 = _ffn_moba(xp, *f2, fn, *moba_s, half, bs - half, final_norm=last)
        sp_l.append(s_p)

        og_s, s_stack = _gla_sample(gq_s, gk_s, la_s, gv_s, sgr_s, gn, state_gla, s_stack, l)
        om_s = jnp.concatenate([om_s0, om_s1], axis=0)
        xs = _outproj(xs, og_s, om_s, sa_s, sb_s, *w_o, l, tm=tm_s)
        xs = _ffn(xs, *f2, fn, l, tm=tm_s, final_norm=last)
        ks_l.append(mk_s.reshape(bs, 1, MOBA_HEADS, MOBA_HD))
        vs_l.append(mv_s.reshape(bs, 1, MOBA_HEADS, MOBA_HD))

    kv_shape = (depth, batch, t, MOBA_HEADS, MOBA_HD)
    return (xp.reshape(batch, t, D_MODEL), xs.reshape(bs, 1, D_MODEL),
            kv_stack[0].reshape(kv_shape), kv_stack[1].reshape(kv_shape), jnp.stack(sp_l),
            jnp.stack(ks_l), jnp.stack(vs_l), s_stack)
```

```python
import functools

import jax
import jax.numpy as jnp
from jax import lax
from jax.experimental import pallas as pl
from jax.experimental.pallas import tpu as pltpu

F32 = jnp.float32
BF16 = jnp.bfloat16

D_MODEL = 1024
GLA_HEADS = 4
GLA_DK = D_MODEL // 2
GLA_DV = D_MODEL
GLA_HDK = GLA_DK // GLA_HEADS
GLA_HDV = GLA_DV // GLA_HEADS
GLA_GATE_RANK = 16
GLA_GATE_TEMP = 16.0
GLA_LOGA_MIN = -4.0
MOBA_HEADS = 8
MOBA_HD = D_MODEL // MOBA_HEADS
MOBA_D = MOBA_HEADS * MOBA_HD
MOBA_BLOCK = 256
MOBA_TOPK = 3
ROT_DIM = MOBA_HD // 4
ROPE_THETA = 500000.0
D_FF = 2816
FFN_RES = 0.5
NORM_EPS = 1e-6

LANES = 128
FFN_CHUNK = 256
GLA_CHUNK = 32
GLA_MID = GLA_CHUNK // 2 - 1
GLA_UNROLL = 32
GLA_SAMPLE_ROWS = 8
MOBA_PROMPT_HEADS = 4
MOBA_SUBPAGES = 4
LOG2E = 1.4426950408889634
NEG = -1e30
VMEM_LIMIT = 56 * 1024 * 1024
VMEM_LIMIT_FUSED = 62 * 1024 * 1024

_NT = (((1,), (1,)), ((), ()))
_TN = (((0,), (0,)), ((), ()))


def _rms(x, g):
    return x * lax.rsqrt(jnp.mean(x * x, axis=-1, keepdims=True) + NORM_EPS) * g


def _dot(a, b):
    return jnp.dot(a, b, preferred_element_type=F32)


def _resident(shape):
    nd = len(shape)
    return pl.BlockSpec(shape, lambda *_: (0,) * nd, pipeline_mode=pl.Buffered(1))


def _layer_resident(stack, layer):
    return pl.BlockSpec((None,) + stack.shape[1:], lambda *_: (layer, 0, 0),
                        pipeline_mode=pl.Buffered(1))


def _params(*sem):
    return pltpu.CompilerParams(dimension_semantics=sem, vmem_limit_bytes=VMEM_LIMIT)


def _ffn_kernel(x_ref, g_ref, wg_ref, wu_ref, wd_ref, fn_ref, o_ref, a_scr, *, final_norm):
    x = x_ref[...]
    h = _rms(x, g_ref[...]).astype(BF16)
    for c in range(D_FF // FFN_CHUNK):
        sl = slice(c * FFN_CHUNK, (c + 1) * FFN_CHUNK)
        g = _dot(h, wg_ref[:, sl])
        u = _dot(h, wu_ref[:, sl])
        a_scr[:, sl] = (g * jax.nn.sigmoid(g) * u).astype(BF16)
    y = x + FFN_RES * _dot(a_scr[...], wd_ref[...])
    if final_norm:
        y = _rms(y, fn_ref[...])
    o_ref[...] = y


def _ffn(x, g, wg, wu, wd, fn, layer, *, tm, final_norm):
    n = x.shape[0]
    row = pl.BlockSpec((tm, D_MODEL), lambda i: (i, 0))
    return pl.pallas_call(
        functools.partial(_ffn_kernel, final_norm=final_norm),
        grid=(n // tm,),
        in_specs=[row] + [_layer_resident(a, layer) for a in (g, wg, wu, wd)]
        + [_resident((1, D_MODEL))],
        out_specs=row,
        out_shape=jax.ShapeDtypeStruct((n, D_MODEL), F32),
        scratch_shapes=[pltpu.VMEM((tm, D_FF), BF16)],
        compiler_params=_params("parallel"),
        name="ffn",
    )(x, g, wg, wu, wd, fn)


def _rope(y, ra, rb1, rb2):
    half = ROT_DIM // 2
    return (y * ra + pltpu.roll(y, LANES - half, 1) * rb1 + pltpu.roll(y, half, 1) * rb2)


def _inproj_kernel(x_ref, g_ref, wgq_ref, wgk_ref, wgv_ref, wgr_ref, wga_ref, wal_ref, bal_ref,
                   wmq_ref, wmk_ref, wmv_ref, wa_ref, wb_ref, ra_ref, rb1_ref, rb2_ref, tri_ref,
                   *rest, cum):
    gq_ref, gk_ref, gv_ref, sgr_ref, bl_ref, mq_ref, mk_ref, mv_ref, sa_ref, sb_ref = rest[-10:]
    h = _rms(x_ref[...], g_ref[...]).astype(BF16)
    ga = _dot(h, wga_ref[...]).astype(BF16)
    z = _dot(ga, wal_ref[...]) + bal_ref[...]
    log_a = (jnp.minimum(z, 0.0) - jnp.log(1.0 + jnp.exp(-jnp.abs(z)))) * (1.0 / GLA_GATE_TEMP)
    log_a = jnp.maximum(log_a, GLA_LOGA_MIN)
    sa_ref[...] = jax.nn.sigmoid(_dot(h, wa_ref[...]))
    sb_ref[...] = jax.nn.sigmoid(_dot(h, wb_ref[...]))
    gr = _dot(h, wgr_ref[...])
    sgr_ref[...] = gr * jax.nn.sigmoid(gr)
    if cum:
        hi = log_a.astype(BF16)
        lo = (log_a - hi.astype(F32)).astype(BF16)
        tri = tri_ref[...]
        bl_ref[...] = _dot(tri, hi) + _dot(tri, lo)
    else:
        bl_ref[...] = log_a
    ra, rb1, rb2 = ra_ref[...], rb1_ref[...], rb2_ref[...]
    mq = _dot(h, wmq_ref[...])
    mk = _dot(h, wmk_ref[...])
    for hd in range(MOBA_HEADS):
        sl = slice(hd * MOBA_HD, (hd + 1) * MOBA_HD)
        mq_ref[:, sl] = _rope(mq[:, sl], ra, rb1, rb2)
        mk_ref[:, sl] = _rope(mk[:, sl], ra, rb1, rb2)
    gq_ref[...] = _dot(h, wgq_ref[...])
    gk_ref[...] = _dot(h, wgk_ref[...])
    gv_ref[...] = _dot(h, wgv_ref[...])
    mv_ref[...] = _dot(h, wmv_ref[...])


def _inproj(x, g, w, rope, tri, *, tm, n_pos_tiles, cum, kv_stack=None, layer=0, depth=None):
    n = x.shape[0]
    row = lambda width: pl.BlockSpec((tm, width), lambda i: (i, 0))
    pos = pl.BlockSpec((tm, LANES), lambda i: (i % n_pos_tiles, 0))
    widths = (GLA_DK, GLA_DK, GLA_DV, GLA_DV, GLA_DK, MOBA_D, MOBA_D, MOBA_D, D_MODEL, D_MODEL)
    out_specs = [row(wd) for wd in widths]
    out_shape = [jax.ShapeDtypeStruct((n, wd), F32) for wd in widths]
    in_specs = ([row(D_MODEL), _layer_resident(g, layer)] + [_resident(a.shape) for a in w]
                + [pos, pos, pos, _resident(tri.shape)])
    args = [x, g, *w, *rope, tri]
    aliases = {}
    if depth is not None:
        for o_idx in (6, 7):
            out_specs[o_idx] = pl.BlockSpec((None, tm, MOBA_D), lambda i: (layer, i, 0))
            out_shape[o_idx] = jax.ShapeDtypeStruct((depth, n, MOBA_D), F32)
        if kv_stack is not None:
            for o_idx, a in zip((6, 7), kv_stack):
                in_specs.append(pl.BlockSpec(memory_space=pl.ANY))
                args.append(a)
                aliases[len(args) - 1] = o_idx
    return pl.pallas_call(
        functools.partial(_inproj_kernel, cum=cum),
        grid=(n // tm,),
        in_specs=in_specs,
        out_specs=out_specs,
        out_shape=out_shape,
        input_output_aliases=aliases,
        compiler_params=_params("parallel"),
        name="inproj",
    )(*args)


def _gla_prompt_kernel(q_ref, k_ref, b_ref, v_ref, sgr_ref, gn_ref, o_ref, s_ref, st_scr):
    t = q_ref.shape[0]
    c_len = GLA_CHUNK
    st_scr[...] = jnp.zeros_like(st_scr)
    causal = (lax.broadcasted_iota(jnp.int32, (c_len, c_len), 0)
              >= lax.broadcasted_iota(jnp.int32, (c_len, c_len), 1))
    scale = GLA_HDK ** -0.5
    gn = gn_ref[...]

    def decayed(rows):
        q, k, b = q_ref[rows, :], k_ref[rows, :], b_ref[rows, :]
        b_mid = b[GLA_MID:GLA_MID + 1, :]
        b_last = b[c_len - 1:c_len, :]
        qt = q * jnp.exp(b - b_mid) * scale
        kt = k * jnp.exp(b_mid - b)
        att = lax.dot_general(qt.astype(BF16), kt.astype(BF16), _NT, preferred_element_type=F32)
        att = jnp.where(causal, att, 0.0).astype(BF16)
        qs = (qt * jnp.exp(b_mid)).astype(BF16)
        ke = (kt * jnp.exp(b_last - b_mid)).astype(BF16)
        return att, qs, ke, jnp.exp(b_last)

    def local(rows, att, ke):
        vb = v_ref[rows, :].astype(BF16)
        return _dot(att, vb), lax.dot_general(vb, ke, _TN, preferred_element_type=F32)

    def body(i, carry):
        rows = [pl.ds(pl.multiple_of((i * GLA_UNROLL + u) * c_len, c_len), c_len)
                for u in range(GLA_UNROLL)]
        pre = [decayed(r) for r in rows]
        loc = [local(r, att, ke) for r, (att, _, ke, _) in zip(rows, pre)]
        st = st_scr[...]
        for r, (_, qs, _, d_last), (o_intra, kv) in zip(rows, pre, loc):
            o = o_intra + lax.dot_general(qs, st.astype(BF16), _NT, preferred_element_type=F32)
            o_ref[r, :] = _rms(o, gn) * sgr_ref[r, :]
            st = st * d_last + kv
        st_scr[...] = st
        return carry

    lax.fori_loop(0, t // (c_len * GLA_UNROLL), body, 0)
    s_ref[0, 0] = st_scr[...].T


def _gla_prompt(gq, gk, bl, gv, sgr, gn, *, batch, t):
    n = gq.shape[0]
    kblk = pl.BlockSpec((t, GLA_HDK), lambda b, h: (b, h))
    vblk = pl.BlockSpec((t, GLA_HDV), lambda b, h: (b, h))
    return pl.pallas_call(
        _gla_prompt_kernel,
        grid=(batch, GLA_HEADS),
        in_specs=[kblk, kblk, kblk, vblk, vblk, pl.BlockSpec((1, GLA_HDV), lambda b, h: (0, 0))],
        out_specs=[vblk, pl.BlockSpec((1, 1, GLA_HDK, GLA_HDV), lambda b, h: (b, h, 0, 0))],
        out_shape=[jax.ShapeDtypeStruct((n, GLA_DV), F32),
                   jax.ShapeDtypeStruct((batch, GLA_HEADS, GLA_HDK, GLA_HDV), F32)],
        scratch_shapes=[pltpu.VMEM((GLA_HDV, GLA_HDK), F32)],
        compiler_params=_params("parallel", "parallel"),
        name="gla_prompt",
    )(gq, gk, bl, gv, sgr, gn)


def _column(r, width):
    col = jnp.broadcast_to(r, (LANES, LANES)).T
    return jnp.concatenate([col] * (width // LANES), axis=1)


def _gla_sample_kernel(q_ref, k_ref, la_ref, v_ref, sgr_ref, gn_ref, s_ref, *rest):
    o_ref, sn_ref = rest[-2:]
    scale = GLA_HDK ** -0.5
    gn = gn_ref[...]
    for r in range(q_ref.shape[1]):
        row = slice(r, r + 1)
        for h in range(GLA_HEADS):
            ks = slice(h * GLA_HDK, (h + 1) * GLA_HDK)
            vs = slice(h * GLA_HDV, (h + 1) * GLA_HDV)
            q, k, la = q_ref[0, row, ks], k_ref[0, row, ks], la_ref[0, row, ks]
            v = v_ref[0, row, vs]
            s_new = _column(jnp.exp(la), GLA_HDV) * s_ref[r, h] + _column(k, GLA_HDV) * v
            sn_ref[r, h] = s_new
            q8 = jnp.broadcast_to(q * scale, (8, GLA_HDK)).astype(BF16)
            o = _dot(q8, s_new.astype(BF16))[0:1, :]
            o_ref[0, row, vs] = _rms(o, gn) * sgr_ref[0, row, vs]


def _gla_sample(gq, gk, la, gv, sgr, gn, state_all, state_out, layer):
    bs = gq.shape[0]
    rows = GLA_SAMPLE_ROWS
    r3 = lambda a: a.reshape(bs // rows, rows, a.shape[-1])
    kblk = pl.BlockSpec((1, rows, GLA_DK), lambda b: (b, 0, 0))
    vblk = pl.BlockSpec((1, rows, GLA_DV), lambda b: (b, 0, 0))
    sblk = pl.BlockSpec((None, rows, GLA_HEADS, GLA_HDK, GLA_HDV), lambda b: (layer, b, 0, 0, 0))
    in_specs = [kblk, kblk, kblk, vblk, vblk, pl.BlockSpec((1, GLA_HDV), lambda b: (0, 0)), sblk]
    args = [r3(gq), r3(gk), r3(la), r3(gv), r3(sgr), gn, state_all]
    aliases = {}
    if state_out is not None:
        in_specs.append(pl.BlockSpec(memory_space=pl.ANY))
        args.append(state_out)
        aliases = {len(args) - 1: 1}
    o, s_new = pl.pallas_call(
        _gla_sample_kernel,
        grid=(bs // rows,),
        in_specs=in_specs,
        out_specs=[vblk, sblk],
        out_shape=[jax.ShapeDtypeStruct((bs // rows, rows, GLA_DV), F32),
                   jax.ShapeDtypeStruct(state_all.shape, F32)],
        input_output_aliases=aliases,
        compiler_params=_params("parallel"),
        name="gla_sample",
    )(*args)
    return o.reshape(bs, GLA_DV), s_new


def _topk_mask(gate, n_valid, blk_idx):
    rank = jnp.zeros_like(gate)
    for jp in range(n_valid):
        row = gate[jp:jp + 1, :]
        ge = jnp.where(row >= gate, 1.0, 0.0)
        gt = jnp.where(row > gate, 1.0, 0.0)
        rank = rank + jnp.where(blk_idx > jp, ge, jnp.where(blk_idx < jp, gt, 0.0))
    return jnp.where(rank < MOBA_TOPK, 1.0, 0.0)


def _moba_prompt_kernel(q_ref, k_ref, v_ref, o_ref):
    t = q_ref.shape[0]
    blk = MOBA_BLOCK
    nb = t // blk
    n_heads = q_ref.shape[1] // MOBA_HD
    scale = MOBA_HD ** -0.5
    lanes = [slice(hh * MOBA_HD, (hh + 1) * MOBA_HD) for hh in range(n_heads)]
    ks = [k_ref[:, sl] for sl in lanes]
    kbs = [k.astype(BF16) for k in ks]
    vts = [v_ref[:, sl].T.astype(BF16) for sl in lanes]
    means_h = [jnp.sum(k.reshape(nb, blk, MOBA_HD), axis=1) * (1.0 / blk) for k in ks]
    causal_t = (lax.broadcasted_iota(jnp.int32, (blk, blk), 0)
                <= lax.broadcasted_iota(jnp.int32, (blk, blk), 1))
    blk_idx = lax.broadcasted_iota(jnp.int32, (nb, blk), 0)

    def masked_scores(i, hh):
        q = q_ref[i * blk:(i + 1) * blk, lanes[hh]]
        qb = (q * (scale * LOG2E)).astype(BF16)
        s = lax.dot_general(kbs[hh][:(i + 1) * blk], qb, _NT,
                            preferred_element_type=F32)
        pieces = []
        if i > MOBA_TOPK:
            gate = lax.dot_general(means_h[hh], q, _NT, preferred_element_type=F32,
                                   precision=lax.Precision.HIGHEST)
            sel = _topk_mask(gate, i, blk_idx)
            for j in range(i):
                bias = jnp.where(sel[j:j + 1, :] > 0.5, 0.0, NEG)
                pieces.append(s[j * blk:(j + 1) * blk] + bias)
        else:
            for j in range(i):
                pieces.append(s[j * blk:(j + 1) * blk])
        pieces.append(jnp.where(causal_t, s[i * blk:], NEG))
        return jnp.concatenate(pieces, axis=0) if len(pieces) > 1 else pieces[0]

    def weighted_values(i, hh, p, l):
        acc = _dot(vts[hh][:, :(i + 1) * blk], p)
        o_ref[i * blk:(i + 1) * blk, lanes[hh]] = (acc / l).T

    items = [(i, hh) for i in range(nb) for hh in range(n_heads)]
    sm_next = masked_scores(*items[0])
    pending = None
    for n, item in enumerate(items):
        sm = sm_next
        if n + 1 < len(items):
            sm_next = masked_scores(*items[n + 1])
        m = jnp.max(sm, axis=0, keepdims=True)
        p = jnp.exp2(sm - m)
        l = jnp.sum(p, axis=0, keepdims=True)
        if pending is not None:
            weighted_values(*pending)
        pending = (*item, p.astype(BF16), l)
    weighted_values(*pending)


def _moba_prompt(mq, mk_stack, mv_stack, layer, *, batch, t):
    n = mq.shape[0]
    width = MOBA_PROMPT_HEADS * MOBA_HD
    blk = pl.BlockSpec((t, width), lambda b, h: (b, h))
    kv = pl.BlockSpec((None, t, width), lambda b, h: (layer, b, h))
    return pl.pallas_call(
        _moba_prompt_kernel,
        grid=(batch, MOBA_HEADS // MOBA_PROMPT_HEADS),
        in_specs=[blk, kv, kv],
        out_specs=blk,
        out_shape=jax.ShapeDtypeStruct((n, MOBA_D), F32),
        compiler_params=_params("parallel", "parallel"),
        name="moba_prompt",
    )(mq, mk_stack, mv_stack)


_HEAD_TILE = (MOBA_HEADS, MOBA_HD)


def _page_partials(kp, vp, q2):
    s = jnp.sum(kp * q2[None], axis=-1, keepdims=True)
    m = jnp.max(s, axis=0)
    e = jnp.exp2(s - m[None])
    return (jnp.broadcast_to(m, _HEAD_TILE), jnp.broadcast_to(jnp.sum(e, axis=0), _HEAD_TILE),
            jnp.sum(e * vp, axis=0), jnp.sum(kp, axis=0))


def _moba_combine(q, q2, k_new, v_new, partials, pages_per_block, page_len):
    m_pg, l_pg, o_pg, ks_pg = (jnp.stack([p[i] for p in partials]) for i in range(4))
    npg = len(partials)
    nblk = npg // pages_per_block
    ksum = ks_pg.reshape(nblk, pages_per_block, *_HEAD_TILE)
    means = jnp.sum(ksum, axis=1) * (1.0 / (pages_per_block * page_len))
    gate = jnp.broadcast_to(jnp.sum(means * q[None], axis=-1, keepdims=True), (nblk,) + _HEAD_TILE)
    bidx = lax.broadcasted_iota(jnp.int32, (nblk,) + _HEAD_TILE, 0)
    rank = jnp.zeros_like(gate)
    for jp in range(nblk):
        g = gate[jp][None]
        ge = jnp.where(g >= gate, 1.0, 0.0)
        gt = jnp.where(g > gate, 1.0, 0.0)
        rank = rank + jnp.where(bidx > jp, ge, jnp.where(bidx < jp, gt, 0.0))
    sel = jnp.where(rank < MOBA_TOPK, 1.0, 0.0)
    sel = jnp.broadcast_to(sel[:, None], (nblk, pages_per_block) + _HEAD_TILE)
    sel = sel.reshape((npg,) + _HEAD_TILE) > 0.5
    s_self = jnp.broadcast_to(jnp.sum(q2 * k_new, axis=-1, keepdims=True), _HEAD_TILE)
    mp = jnp.where(sel, m_pg, NEG)
    m_all = jnp.maximum(jnp.max(mp, axis=0), s_self)
    w = jnp.where(sel, jnp.exp2(mp - m_all[None]), 0.0)
    w_self = jnp.exp2(s_self - m_all)
    l_all = w_self + jnp.sum(w * l_pg, axis=0)
    o_all = w_self * v_new + jnp.sum(w * o_pg, axis=0)
    return o_all / l_all


def _ffn_moba_kernel(pt_ref, x_ref, g_ref, wg_ref, wu_ref, wd_ref, fn_ref, q_ref, kn_ref, vn_ref,
                     *refs, n_pages, pages_per_block, final_norm):
    del pt_ref
    kp_refs = refs[:n_pages]
    vp_refs = refs[n_pages:2 * n_pages]
    y_ref, o_ref, a_scr = refs[2 * n_pages:]
    x = x_ref[...]
    h = _rms(x, g_ref[...]).astype(BF16)
    q = q_ref[0]
    q2 = q * (MOBA_HD ** -0.5 * LOG2E)
    n_chunks = D_FF // FFN_CHUNK
    page_len = kp_refs[0].shape[0]
    sub = page_len // MOBA_SUBPAGES
    partials = []
    for c in range(max(n_chunks, n_pages)):
        if c < n_chunks:
            sl = slice(c * FFN_CHUNK, (c + 1) * FFN_CHUNK)
            gate = _dot(h, wg_ref[:, sl])
            up = _dot(h, wu_ref[:, sl])
            a_scr[:, sl] = (gate * jax.nn.sigmoid(gate) * up).astype(BF16)
        if c < n_pages:
            for r in range(0, page_len, sub):
                partials.append(_page_partials(kp_refs[c][r:r + sub], vp_refs[c][r:r + sub], q2))
    y = x + FFN_RES * _dot(a_scr[...], wd_ref[...])
    if final_norm:
        y = _rms(y, fn_ref[...])
    y_ref[...] = y
    o_ref[0] = _moba_combine(q, q2, kn_ref[0], vn_ref[0], partials,
                             pages_per_block * (page_len // sub), sub)


def _ffn_moba(x, g, wg, wu, wd, fn, mq, mk, mv, cache_k, cache_v, page_table, layer, seq0, n_seq,
              *, final_norm):
    n = x.shape[0]
    bs, n_pages = page_table.shape
    page = cache_k.shape[2]
    assert n % n_seq == 0 and (n // n_seq) % 8 == 0
    tm = n // n_seq
    r3 = lambda a: a.reshape(bs, MOBA_HEADS, MOBA_HD)
    row = pl.BlockSpec((tm, D_MODEL), lambda i, pt: (i, 0))
    tok = pl.BlockSpec((1, MOBA_HEADS, MOBA_HD), lambda i, pt: (seq0 + i, 0, 0))

    def page_spec(p):
        return pl.BlockSpec((None, None, page, MOBA_HEADS, MOBA_HD),
                            lambda i, pt: (layer, pt[(seq0 + i) * n_pages + p], 0, 0, 0))

    pages = [page_spec(p) for p in range(n_pages)]
    y, o = pl.pallas_call(
        functools.partial(_ffn_moba_kernel, n_pages=n_pages, pages_per_block=MOBA_BLOCK // page,
                          final_norm=final_norm),
        grid_spec=pltpu.PrefetchScalarGridSpec(
            num_scalar_prefetch=1,
            grid=(n_seq,),
            in_specs=[row] + [_layer_resident(a, layer) for a in (g, wg, wu, wd)]
            + [_resident((1, D_MODEL)), tok, tok, tok] + pages + pages,
            out_specs=[row, pl.BlockSpec((1, MOBA_HEADS, MOBA_HD), lambda i, pt: (i, 0, 0))],
            scratch_shapes=[pltpu.VMEM((tm, D_FF), BF16)]),
        out_shape=[jax.ShapeDtypeStruct((n, D_MODEL), F32),
                   jax.ShapeDtypeStruct((n_seq, MOBA_HEADS, MOBA_HD), F32)],
        compiler_params=pltpu.CompilerParams(dimension_semantics=("parallel",),
                                             vmem_limit_bytes=VMEM_LIMIT_FUSED),
        name="ffn_moba",
    )(page_table.reshape(-1), x, g, wg, wu, wd, fn, r3(mq), r3(mk), r3(mv),
      *([cache_k] * n_pages), *([cache_v] * n_pages))
    return y, o.reshape(n_seq, MOBA_D)


def _outproj_kernel(x_ref, og_ref, om_ref, sa_ref, sb_ref, wg_ref, wm_ref, wo_ref, o_ref):
    ya = _dot(og_ref[...].astype(BF16), wg_ref[...])
    yb = _dot(om_ref[...].astype(BF16), wm_ref[...])
    merged = sa_ref[...] * ya + sb_ref[...] * yb
    o_ref[...] = x_ref[...] + _dot(merged.astype(BF16), wo_ref[...])


def _outproj(x, og, om, sa, sb, wg, wm, wo, layer, *, tm):
    n = x.shape[0]
    row = pl.BlockSpec((tm, D_MODEL), lambda i: (i, 0))
    return pl.pallas_call(
        _outproj_kernel,
        grid=(n // tm,),
        in_specs=[row] * 5 + [_layer_resident(a, layer) for a in (wg, wm, wo)],
        out_specs=row,
        out_shape=jax.ShapeDtypeStruct((n, D_MODEL), F32),
        compiler_params=_params("parallel"),
        name="outproj",
    )(x, og, om, sa, sb, wg, wm, wo)


def _rope_tables(pos):
    half = ROT_DIM // 2
    inv = ROPE_THETA ** (-jnp.arange(half, dtype=F32) * 2.0 / ROT_DIM)
    ang = pos.astype(F32)[:, None] * inv[None, :]
    cos, sin = jnp.cos(ang), jnp.sin(ang)
    n = pos.shape[0]
    zeros = lambda w: jnp.zeros((n, w), F32)
    ra = jnp.concatenate([cos, cos, jnp.ones((n, MOBA_HD - ROT_DIM), F32)], axis=1)
    rb1 = jnp.concatenate([-sin, zeros(MOBA_HD - half)], axis=1)
    rb2 = jnp.concatenate([zeros(half), sin, zeros(MOBA_HD - ROT_DIM)], axis=1)
    return ra, rb1, rb2


def _chunk_tri(tm):
    r = jnp.arange(tm)
    same = (r[:, None] // GLA_CHUNK) == (r[None, :] // GLA_CHUNK)
    return (same & (r[:, None] >= r[None, :])).astype(BF16)


def _split_w_in(w_in, w_alpha, b_alpha):
    sizes = (GLA_DK, GLA_DK, GLA_DV, GLA_DV, GLA_GATE_RANK, MOBA_D, MOBA_D, MOBA_D, D_MODEL, D_MODEL)
    offs = [0]
    for s in sizes:
        offs.append(offs[-1] + s)
    parts = [w_in[:, offs[i]:offs[i + 1]].astype(BF16) for i in range(len(sizes))]
    wgq, wgk, wgv, wgr, wga, wmq, wmk, wmv, wa, wb = parts
    pad = LANES - GLA_GATE_RANK
    wga = jnp.pad(wga, ((0, 0), (0, pad)))
    wal = jnp.pad(w_alpha.astype(BF16), ((0, pad), (0, 0)))
    return (wgq, wgk, wgv, wgr, wga, wal, b_alpha.reshape(1, GLA_DK), wmq, wmk, wmv, wa, wb)


def kernel(x_prompt, x_sample, cache_k, cache_v, state_gla, page_table, ffn1_norm, ffn1_w_gate, ffn1_w_up, ffn1_w_down, mix_norm, w_in, gla_w_alpha, gla_b_alpha, gla_out_norm, w_o_gla, w_o_moba, w_out, ffn2_norm, ffn2_w_gate, ffn2_w_up, ffn2_w_down, final_norm):
    batch, t, _ = x_prompt.shape
    bs = x_sample.shape[0]
    depth = w_in.shape[0]
    past_len = page_table.shape[1] * cache_k.shape[2]
    tm_p, tm_in = 512, 256
    tm_s = bs

    xp = x_prompt.reshape(batch * t, D_MODEL)
    xs = x_sample.reshape(bs, D_MODEL)
    rope_p = _rope_tables(jnp.arange(t))
    rope_s = _rope_tables(jnp.full((tm_s,), past_len))
    tri_p = _chunk_tri(tm_in)
    tri_s = jnp.zeros((8, LANES), BF16)
    fn = final_norm.reshape(1, D_MODEL)
    row = lambda a: a.reshape(1, -1)

    bf16 = lambda a: a.astype(BF16)
    f1 = (ffn1_norm[:, None, :], bf16(ffn1_w_gate), bf16(ffn1_w_up), bf16(ffn1_w_down))
    f2 = (ffn2_norm[:, None, :], bf16(ffn2_w_gate), bf16(ffn2_w_up), bf16(ffn2_w_down))
    w_o = (bf16(w_o_gla), bf16(w_o_moba), bf16(w_out))
    mixn = mix_norm[:, None, :]

    sp_l, ks_l, vs_l = [], [], []
    kv_stack, s_stack = None, None
    for l in range(depth):
        last = l == depth - 1
        gn = row(gla_out_norm[l])
        w_proj = _split_w_in(w_in[l], gla_w_alpha[l], gla_b_alpha[l])

        xs = _ffn(xs, *f1, fn, l, tm=tm_s, final_norm=False)
        gq_s, gk_s, gv_s, sgr_s, la_s, mq_s, mk_s, mv_s, sa_s, sb_s = _inproj(
            xs, mixn, w_proj, rope_s, tri_s, tm=tm_s, n_pos_tiles=1, cum=False, layer=l)
        moba_s = (mq_s, mk_s, mv_s, cache_k, cache_v, page_table, l)
        half = bs // 2

        xp, om_s0 = _ffn_moba(xp, *f1, fn, *moba_s, 0, half, final_norm=False)
        gq, gk, gv, sgr, bl, mq, mk, mv, sa, sb = _inproj(
            xp, mixn, w_proj, rope_p, tri_p, tm=tm_in, n_pos_tiles=t // tm_in, cum=True,
            kv_stack=kv_stack, layer=l, depth=depth)
        kv_stack = (mk, mv)
        og, s_p = _gla_prompt(gq, gk, bl, gv, sgr, gn, batch=batch, t=t)
        om = _moba_prompt(mq, mk, mv, l, batch=batch, t=t)
        xp = _outproj(xp, og, om, sa, sb, *w_o, l, tm=tm_p)
        xp, om_s1 = _ffn_moba(xp, *f2, fn, *moba_s, half, bs - half, final_norm=last)
        sp_l.append(s_p)

        og_s, s_stack = _gla_sample(gq_s, gk_s, la_s, gv_s, sgr_s, gn, state_gla, s_stack, l)
        om_s = jnp.concatenate([om_s0, om_s1], axis=0)
        xs = _outproj(xs, og_s, om_s, sa_s, sb_s, *w_o, l, tm=tm_s)
        xs = _ffn(xs, *f2, fn, l, tm=tm_s, final_norm=last)
        ks_l.append(mk_s.reshape(bs, 1, MOBA_HEADS, MOBA_HD))
        vs_l.append(mv_s.reshape(bs, 1, MOBA_HEADS, MOBA_HD))

    kv_shape = (depth, batch, t, MOBA_HEADS, MOBA_HD)
    return (xp.reshape(batch, t, D_MODEL), xs.reshape(bs, 1, D_MODEL),
            kv_stack[0].reshape(kv_shape), kv_stack[1].reshape(kv_shape), jnp.stack(sp_l),
            jnp.stack(ks_l), jnp.stack(vs_l), s_stack)
```
